```python
import jax
import jax.numpy as jnp
from jax import lax
import numpy as np

D_MODEL = 1024
BATCH = 8
SEQ = 4096
DEPTH = 4

GRID_W = 64
CTX_LEN = 256
HEAD_DIM = 64
D_MIX = D_MODEL
D_MLSTM = D_MIX // 2
D_NA = D_MIX - D_MLSTM
H_MLSTM = D_MLSTM // HEAD_DIM
H_NA = D_NA // HEAD_DIM
N_GATE = 4 * H_MLSTM
D_IN = 4 * D_MLSTM + 3 * D_NA + N_GATE
MLSTM_CHUNK = 64
NA_ROWS_MAX = 8
NA_COLS = 16
D_FF = 4 * D_MODEL
N_MOD = 6
ROPE_BASE = 10000.0
EPS = 1e-6
F_BIAS_LO = 3.0
F_BIAS_HI = 6.0

kernel_name = "hybrid_mlstm_natten_dit_block"


def rmsnorm(x, g):
    x32 = x.astype(jnp.float32)
    y = x32 * lax.rsqrt(jnp.mean(x32 * x32, axis=-1, keepdims=True) + EPS)
    return (y * g.astype(jnp.float32)).astype(x.dtype)


def sq_relu_mlp(h, w1, w2):
    return jnp.square(jax.nn.relu(h @ w1)) @ w2


def split_heads(a, n_heads):
    b, t, _ = a.shape
    return a.reshape(b, t, n_heads, HEAD_DIM).transpose(0, 2, 1, 3)


def merge_heads(a):
    b, h, t, d = a.shape
    return a.transpose(0, 2, 1, 3).reshape(b, t, h * d)


def head_layernorm(h, g):
    mu = jnp.mean(h, axis=-1, keepdims=True)
    var = jnp.mean(jnp.square(h - mu), axis=-1, keepdims=True)
    return merge_heads((h - mu) * lax.rsqrt(var + EPS)) * g.astype(jnp.float32)


def axial_rope_tables(n_tokens):
    t = jnp.arange(n_tokens)
    row = (t // GRID_W).astype(jnp.float32)
    col = (t % GRID_W).astype(jnp.float32)
    n_freq = HEAD_DIM // 4
    inv_freq = ROPE_BASE ** (-jnp.arange(n_freq, dtype=jnp.float32) / n_freq)
    ang = jnp.concatenate([row[:, None] * inv_freq, col[:, None] * inv_freq], axis=-1)
    return jnp.cos(ang), jnp.sin(ang)


def apply_rope(x, cos, sin):
    x1, x2 = jnp.split(x, 2, axis=-1)
    return jnp.concatenate([x1 * cos - x2 * sin, x1 * sin + x2 * cos], axis=-1)


def mlstm_scan(q, k, v, log_i, log_f, state):
    B, H, T, d = q.shape
    L = MLSTM_CHUNK
    nc = T // L

    def chunks(a):
        return jnp.moveaxis(a.reshape(a.shape[:2] + (nc, L) + a.shape[3:]), 2, 0)

    causal = jnp.tril(jnp.ones((L, L), dtype=bool))

    def step(carry, xs):
        C, n, m = carry
        qc, kc, vc, ic, fc = xs
        b = jnp.cumsum(fc, axis=-1)
        d_intra = jnp.where(causal, b[..., :, None] - b[..., None, :] + ic[..., None, :], -jnp.inf)
        d_prev = b + m[..., None]
        m_t = jnp.maximum(d_prev, jnp.max(d_intra, axis=-1))
        w_intra = jnp.exp(d_intra - m_t[..., None])
        w_prev = jnp.exp(d_prev - m_t)
        s = jnp.einsum('bhtd,bhsd->bhts', qc, kc) * w_intra
        num = w_prev[..., None] * jnp.einsum('bhtd,bhde->bhte', qc, C) + jnp.einsum('bhts,bhse->bhte', s, vc)
        qn = w_prev * jnp.einsum('bhtd,bhd->bht', qc, n) + jnp.sum(s, axis=-1)
        h = num / jnp.maximum(jnp.abs(qn), jnp.exp(-m_t))[..., None]
        b_end = b[..., -1]
        d_end = b_end[..., None] - b + ic
        m_new = jnp.maximum(b_end + m, jnp.max(d_end, axis=-1))
        w_c = jnp.exp(b_end + m - m_new)
        w_s = jnp.exp(d_end - m_new[..., None])
        C = w_c[..., None, None] * C + jnp.einsum('bhs,bhsd,bhse->bhde', w_s, kc, vc)
        n = w_c[..., None] * n + jnp.einsum('bhs,bhsd->bhd', w_s, kc)
        return (C, n, m_new), h

    state, h = lax.scan(step, state, (chunks(q), chunks(k), chunks(v), chunks(log_i), chunks(log_f)))
    return jnp.moveaxis(h, 0, 2).reshape(B, H, T, d), state


def mlstm_group(q, k, v, o, gates, qc, kc, vc, oc, gates_c, b_gate, norm_g, cos, sin, need_ctx):
    f32 = jnp.float32
    scale = HEAD_DIM ** -0.5

    def prep(a):
        return split_heads(a, H_MLSTM).astype(f32)

    qx = apply_rope(prep(q), cos, sin) * scale
    kx = apply_rope(prep(k), cos, sin)
    vx = prep(v)
    qcx = prep(qc) * scale
    kcx = prep(kc)
    vcx = prep(vc)

    def gate_split(g):
        return jnp.split((g.astype(f32) + b_gate.astype(f32)).transpose(0, 2, 1), 4, axis=1)

    i_fw, i_bw, f_fw, f_bw = gate_split(gates)
    ic_fw, ic_bw, fc_fw, fc_bw = gate_split(gates_c)
    B, H, _, d = qx.shape
    state0 = (jnp.zeros((B, H, d, d), f32), jnp.zeros((B, H, d), f32), jnp.zeros((B, H), f32))

    def direction(rev, ig, fg, igc, fgc):
        flip = (lambda a: jnp.flip(a, axis=2)) if rev else (lambda a: a)
        hc, st = mlstm_scan(flip(qcx), flip(kcx), flip(vcx), flip(igc), jax.nn.log_sigmoid(flip(fgc)), state0)
        hx, _ = mlstm_scan(flip(qx), flip(kx), flip(vx), flip(ig), jax.nn.log_sigmoid(flip(fg)), st)
        return flip(hx), flip(hc)

    hx_f, hc_f = direction(False, i_fw, f_fw, ic_fw, fc_fw)
    hx_b, hc_b = direction(True, i_bw, f_bw, ic_bw, fc_bw)
    y = (head_layernorm(hx_f + hx_b, norm_g) * jax.nn.sigmoid(o.astype(f32))).astype(q.dtype)
    yc = None
    if need_ctx:
        yc = (head_layernorm(hc_f + hc_b, norm_g) * jax.nn.sigmoid(oc.astype(f32))).astype(q.dtype)
    return y, yc


def neighborhood_attention(q, k, v, qc, kc, vc, rpb, rows, need_ctx):
    B, S, _ = q.shape
    scale = HEAD_DIM ** -0.5
    kr_n = min(NA_ROWS_MAX, rows)

    def grid(a):
        return split_heads(a, H_NA).reshape(B, H_NA, rows, GRID_W, HEAD_DIM)

    qg = grid(q * scale)
    kg = grid(k)
    vg = grid(v)
    kch = split_heads(kc, H_NA)
    vch = split_heads(vc, H_NA)
    cols = jnp.arange(GRID_W)
    col_start = jnp.clip(cols - NA_COLS // 2, 0, GRID_W - NA_COLS)
    col_mask = (cols[None, :] >= col_start[:, None]) & (cols[None, :] < col_start[:, None] + NA_COLS)
    dc_idx = jnp.clip(cols[None, :] - cols[:, None] + NA_COLS - 1, 0, 2 * NA_COLS - 2)
    rpb_cols = rpb.astype(jnp.float32)[:, :, dc_idx]

    def row_block(r):
        rs = jnp.clip(r - kr_n // 2, 0, rows - kr_n)
        q_r = lax.dynamic_index_in_dim(qg, r, axis=2, keepdims=False)
        k_r = lax.dynamic_slice_in_dim(kg, rs, kr_n, axis=2)
        v_r = lax.dynamic_slice_in_dim(vg, rs, kr_n, axis=2)
        bias = jnp.take(rpb_cols, rs + jnp.arange(kr_n) - r + NA_ROWS_MAX - 1, axis=1)
        s_loc = jnp.einsum('bhqd,bhrkd->bhqrk', q_r, k_r).astype(jnp.float32) + bias.transpose(0, 2, 1, 3)
        s_loc = jnp.where(col_mask[:, None, :], s_loc, -jnp.inf)
        s_ctx = jnp.einsum('bhqd,bhcd->bhqc', q_r, kch).astype(jnp.float32)
        s = jnp.concatenate([s_loc.reshape(B, H_NA, GRID_W, kr_n * GRID_W), s_ctx], axis=-1)
        p = jax.nn.softmax(s, axis=-1).astype(v.dtype)
        p_loc = p[..., :kr_n * GRID_W].reshape(B, H_NA, GRID_W, kr_n, GRID_W)
        p_ctx = p[..., kr_n * GRID_W:]
        return jnp.einsum('bhqrk,bhrkd->bhqd', p_loc, v_r) + jnp.einsum('bhqc,bhcd->bhqd', p_ctx, vch)

    out = lax.map(row_block, jnp.arange(rows))
    y = out.transpose(1, 0, 3, 2, 4).reshape(B, S, H_NA * HEAD_DIM)
    yc = None
    if need_ctx:
        qch = split_heads(qc * scale, H_NA)
        sc = jnp.einsum('bhqd,bhkd->bhqk', qch, kch).astype(jnp.float32)
        pc = jax.nn.softmax(sc, axis=-1).astype(vc.dtype)
        yc = merge_heads(jnp.einsum('bhqk,bhkd->bhqd', pc, vch))
    return y, yc


def token_mixers(p, pc, b_gate, norm_g, rpb, cos, sin, rows, need_ctx):
    cuts = [int(i) for i in np.cumsum([D_MLSTM] * 4 + [D_NA] * 3)]
    qm, km, vm, om, qn, kn, vn, gm = jnp.split(p, cuts, axis=-1)
    qmc, kmc, vmc, omc, qnc, knc, vnc, gmc = jnp.split(pc, cuts, axis=-1)
    y_m, yc_m = mlstm_group(qm, km, vm, om, gm, qmc, kmc, vmc, omc, gmc, b_gate, norm_g, cos, sin, need_ctx)
    y_n, yc_n = neighborhood_attention(qn, kn, vn, qnc, knc, vnc, rpb, rows, need_ctx)
    y = jnp.concatenate([y_m, y_n], axis=-1)
    yc = jnp.concatenate([yc_m, yc_n], axis=-1) if need_ctx else None
    return y, yc


def setup_inputs(seed: int = 0) -> dict:
    key = jax.random.key(seed)
    ks = jax.random.split(key, 16)
    nrm = jax.random.normal
    f32 = jnp.float32
    x = nrm(ks[0], (BATCH, SEQ, D_MODEL), f32)
    c = nrm(ks[1], (BATCH, D_MODEL), f32)
    ctx = nrm(ks[2], (BATCH, CTX_LEN, D_MODEL), f32)
    c_ctx = nrm(ks[3], (D_MODEL,), f32)
    w_ada = nrm(ks[4], (DEPTH, D_MODEL, N_MOD * D_MODEL), f32) * (0.5 * D_MODEL ** -0.5)
    b_ada = 0.02 * nrm(ks[5], (DEPTH, N_MOD * D_MODEL), f32)
    norm1_g = 1.0 + 0.02 * nrm(ks[6], (DEPTH, D_MODEL), f32)
    w_in = nrm(ks[7], (DEPTH, D_MODEL, D_IN), f32) * D_MODEL ** -0.5
    i_bias = 0.1 * nrm(ks[8], (DEPTH, 2 * H_MLSTM), f32)
    f_base = jnp.tile(jnp.linspace(F_BIAS_LO, F_BIAS_HI, H_MLSTM, dtype=f32), 2)
    f_bias = f_base[None, :] + 0.1 * nrm(ks[9], (DEPTH, 2 * H_MLSTM), f32)
    b_gate = jnp.concatenate([i_bias, f_bias], axis=-1)
    mlstm_norm_g = 1.0 + 0.02 * nrm(ks[10], (DEPTH, D_MLSTM), f32)
    rpb = 0.1 * nrm(ks[11], (DEPTH, H_NA, 2 * NA_ROWS_MAX - 1, 2 * NA_COLS - 1), f32)
    w_out = nrm(ks[12], (DEPTH, D_MIX, D_MODEL), f32) * D_MIX ** -0.5
    norm2_g = 1.0 + 0.02 * nrm(ks[13], (DEPTH, D_MODEL), f32)
    km1, km2 = jax.random.split(ks[14])
    w_mlp1 = nrm(km1, (DEPTH, D_MODEL, D_FF), f32) * D_MODEL ** -0.5
    w_mlp2 = nrm(km2, (DEPTH, D_FF, D_MODEL), f32) * D_FF ** -0.5
    final_g = 1.0 + 0.02 * nrm(ks[15], (D_MODEL,), f32)
    return {"x": x, "c": c, "ctx": ctx, "c_ctx": c_ctx, "w_ada": w_ada, "b_ada": b_ada,
            "norm1_g": norm1_g, "w_in": w_in, "b_gate": b_gate, "mlstm_norm_g": mlstm_norm_g,
            "rpb": rpb, "w_out": w_out, "norm2_g": norm2_g, "w_mlp1": w_mlp1, "w_mlp2": w_mlp2,
            "final_g": final_g}


def reference(x, c, ctx, c_ctx, w_ada, b_ada, norm1_g, w_in, b_gate, mlstm_norm_g, rpb, w_out,
              norm2_g, w_mlp1, w_mlp2, final_g):
    B, S, D = x.shape
    rows = S // GRID_W
    cos, sin = axial_rope_tables(S)
    silu_c = jax.nn.silu(c)
    silu_cc = jax.nn.silu(c_ctx)
    xc = ctx
    for l in range(DEPTH):
        need_ctx = l < DEPTH - 1
        mod = silu_c @ w_ada[l] + b_ada[l]
        mod_c = silu_cc @ w_ada[l] + b_ada[l]
        sh1, sc1, g1, sh2, sc2, g2 = jnp.split(mod[:, None, :], N_MOD, axis=-1)
        sh1c, sc1c, g1c, sh2c, sc2c, g2c = jnp.split(mod_c, N_MOD, axis=-1)
        h = rmsnorm(x, norm1_g[l]) * (1.0 + sc1) + sh1
        hc = rmsnorm(xc, norm1_g[l]) * (1.0 + sc1c) + sh1c
        y, yc = token_mixers(h @ w_in[l], hc @ w_in[l], b_gate[l], mlstm_norm_g[l], rpb[l],
                             cos, sin, rows, need_ctx)
        x = x + g1 * (y @ w_out[l])
        h = rmsnorm(x, norm2_g[l]) * (1.0 + sc2) + sh2
        x = x + g2 * sq_relu_mlp(h, w_mlp1[l], w_mlp2[l])
        if need_ctx:
            xc = xc + g1c * (yc @ w_out[l])
            hc = rmsnorm(xc, norm2_g[l]) * (1.0 + sc2c) + sh2c
            xc = xc + g2c * sq_relu_mlp(hc, w_mlp1[l], w_mlp2[l])
    return rmsnorm(x, final_g)
```

```python
import functools

import jax
import jax.numpy as jnp
import numpy as np
from jax import lax
from jax.experimental import pallas as pl
from jax.experimental.pallas import tpu as pltpu

HEAD_DIM = 64
GRID_W = 64
NA_ROWS = 8
NA_COLS = 16
N_MOD = 6
ROPE_BASE = 10000.0
EPS = 1e-6

LANES = 128
MLSTM_L = 128
NA_QROWS = 4
NA_WIN = NA_QROWS + NA_ROWS
NEG = -1e30
COL_GROUP = 512
VMEM_LIMIT = 56 * 1024 * 1024

F32 = jnp.float32
BF16 = jnp.bfloat16


def _dot(a, b):
    return jnp.dot(a, b, preferred_element_type=F32)


def _dot_nt(a, b):
    return lax.dot_general(a, b, (((1,), (1,)), ((), ())), preferred_element_type=F32)


def _split_dot(a, b, terms):
    out = None
    rem = a
    for t in range(terms):
        piece = rem.astype(BF16)
        part = _dot(piece, b)
        out = part if out is None else out + part
        if t + 1 < terms:
            rem = rem - piece.astype(F32)
    return out


def _sigmoid(v):
    return 1.0 / (1.0 + jnp.exp(-v))


def _log_sigmoid(v):
    return jnp.minimum(v, 0.0) - jnp.log(1.0 + jnp.exp(-jnp.abs(v)))


def _adaln_kernel(c_ref, w_ref, b_ref, o_ref):
    c = c_ref[...]
    s = (c * _sigmoid(c)).astype(BF16)
    o_ref[...] = _dot(s, w_ref[...].astype(BF16)) + b_ref[...]


def _adaln(cc, w_ada, b_ada):
    depth, d, n = w_ada.shape
    rows = cc.shape[0]
    tn = n // 4
    return pl.pallas_call(
        _adaln_kernel,
        grid=(depth, n // tn),
        in_specs=[
            pl.BlockSpec((rows, d), lambda l, j: (0, 0)),
            pl.BlockSpec((None, d, tn), lambda l, j: (l, 0, j)),
            pl.BlockSpec((None, 1, tn), lambda l, j: (l, 0, j)),
        ],
        out_specs=pl.BlockSpec((None, rows, tn), lambda l, j: (l, 0, j)),
        out_shape=jax.ShapeDtypeStruct((depth, rows, n), F32),
        compiler_params=pltpu.CompilerParams(
            dimension_semantics=("arbitrary", "arbitrary"), vmem_limit_bytes=VMEM_LIMIT),
        name="adaln",
    )(cc, w_ada, b_ada.reshape(depth, 1, n))


def _modulated_norm(x, g, scale, shift):
    y = x * lax.rsqrt(jnp.mean(x * x, axis=-1, keepdims=True) + EPS)
    return (y * g) * (1.0 + scale) + shift


def _in_proj_kernel(x_ref, mod_ref, g_ref, w_ref, wgt_ref, bg_ref, cos_ref, sin_ref,
                    p_ref, gr_ref, *, d_model, rope):
    d = d_model
    tm = x_ref.shape[0]
    h = _modulated_norm(x_ref[...], g_ref[...], mod_ref[:, d:2 * d], mod_ref[:, 0:d])
    hb = h.astype(BF16)

    n_groups = w_ref.shape[1] // COL_GROUP
    lane = lax.broadcasted_iota(jnp.int32, (tm, LANES), 1)
    first_half = (lane % HEAD_DIM) < (HEAD_DIM // 2)
    q_scale = HEAD_DIM ** -0.5
    for cg in range(n_groups):
        acc = _dot(hb, w_ref[:, cg * COL_GROUP:(cg + 1) * COL_GROUP])
        is_mlstm_qk = cg in (0, 1)
        is_q = cg in (0, 4)
        for s in range(COL_GROUP // LANES):
            slab = acc[:, s * LANES:(s + 1) * LANES]
            if rope and is_mlstm_qk:
                partner = jnp.where(first_half,
                                    pltpu.roll(slab, LANES - HEAD_DIM // 2, axis=1),
                                    pltpu.roll(slab, HEAD_DIM // 2, axis=1))
                slab = slab * cos_ref[...] + partner * sin_ref[...]
            if is_q:
                slab = slab * q_scale
            c0 = cg * COL_GROUP + s * LANES
            p_ref[:, c0:c0 + LANES] = slab.astype(BF16)

    gt = _dot_nt(wgt_ref[...], hb) + bg_ref[:, 0:1]
    nh = wgt_ref.shape[0] // 4
    u = lax.broadcasted_iota(jnp.int32, (MLSTM_L, MLSTM_L), 0)
    s_ = lax.broadcasted_iota(jnp.int32, (MLSTM_L, MLSTM_L), 1)
    tri_prefix = (u <= s_).astype(BF16)
    tri_suffix = (u >= s_).astype(BF16)
    for j in range(tm // MLSTM_L):
        g = gt[:, j * MLSTM_L:(j + 1) * MLSTM_L]
        lf = _log_sigmoid(g[2 * nh:4 * nh])
        b_f = _split_dot(lf, tri_prefix, 3)[0:nh]
        b_b = _split_dot(lf, tri_suffix, 3)[nh:2 * nh]
        gr_ref[j] = jnp.concatenate(
            [g[0:nh] - b_f, g[nh:2 * nh] - b_b, lf], axis=0)


def _in_proj(x, mod4, mod_row, g, w_main, w_gate_t, b_gate_col, cos_t, sin_t, *, rope):
    b, t, d = x.shape
    n = w_main.shape[1]
    ng = w_gate_t.shape[0]
    tm = min(512, t)
    kern = functools.partial(_in_proj_kernel, d_model=d, rope=rope)
    return pl.pallas_call(
        kern,
        grid=(t // tm, b),
        in_specs=[
            pl.BlockSpec((None, tm, d), lambda i, bi: (bi, i, 0)),
            pl.BlockSpec((None, 1, N_MOD * d), lambda i, bi: (mod_row(bi), 0, 0)),
            pl.BlockSpec((1, d), lambda i, bi: (0, 0)),
            pl.BlockSpec((d, n), lambda i, bi: (0, 0), pipeline_mode=pl.Buffered(1)),
            pl.BlockSpec((ng, d), lambda i, bi: (0, 0)),
            pl.BlockSpec((ng, LANES), lambda i, bi: (0, 0)),
            pl.BlockSpec((tm, LANES), lambda i, bi: (i, 0)),
            pl.BlockSpec((tm, LANES), lambda i, bi: (i, 0)),
        ],
        out_specs=[
            pl.BlockSpec((None, tm, n), lambda i, bi: (bi, i, 0)),
            pl.BlockSpec((None, tm // MLSTM_L, ng, MLSTM_L), lambda i, bi: (bi, i, 0, 0)),
        ],
        out_shape=[
            jax.ShapeDtypeStruct((b, t, n), BF16),
            jax.ShapeDtypeStruct((b, t // MLSTM_L, ng, MLSTM_L), F32),
        ],
        compiler_params=pltpu.CompilerParams(
            dimension_semantics=("arbitrary", "arbitrary"), vmem_limit_bytes=VMEM_LIMIT),
        name="in_proj_rope" if rope else "in_proj",
    )(x, mod4, g, w_main, w_gate_t, b_gate_col, cos_t, sin_t)


def _chunk_rows(j):
    if isinstance(j, int):
        return pl.ds(j * MLSTM_L, MLSTM_L)
    return pl.ds(pl.multiple_of(j * MLSTM_L, MLSTM_L), MLSTM_L)


def _mlstm_chunk(q_ref, k_ref, v_ref, gr_ref, j, hp, c_ref, m_state, hacc_ref, *, rev, nh):
    L = MLSTM_L
    rows = _chunk_rows(j)
    q2 = q_ref[rows, :]
    k2 = k_ref[rows, :]
    v2 = v_ref[rows, :]
    lane = lax.broadcasted_iota(jnp.int32, (L, LANES), 1)
    sub = lax.broadcasted_iota(jnp.int32, (L, LANES), 0)
    head_a = lane < HEAD_DIM
    tri = (lane >= sub) if rev else (lane <= sub)
    end = 0 if rev else L - 1
    k2t = k2.astype(F32).T
    k2t_b = k2t.astype(BF16)
    qc = _dot(q2, c_ref[...].astype(BF16))
    base = nh if rev else 0
    na, eneg, m_new = [], [], []
    for hd in range(2):
        r = 2 * hp + hd
        a = gr_ref[j, pl.ds(base + r, 1), :]
        lf = gr_ref[j, pl.ds(2 * nh + base + r, 1), :]
        m_h = m_state[hd]
        hmask = head_a if hd == 0 else jnp.logical_not(head_a)
        am = jnp.where(tri, a, NEG)
        mrow = jnp.maximum(jnp.max(am, axis=1, keepdims=True), m_h)
        w = jnp.exp(am - mrow)
        bcol = jnp.sum(jnp.where(tri, lf, 0.0), axis=1, keepdims=True)
        wprev = jnp.exp(m_h - mrow)
        eneg.append(jnp.exp(-(bcol + mrow)))
        qh = jnp.where(hmask, q2, jnp.zeros_like(q2))
        s = _dot(qh, k2t_b)
        p = (s * w).astype(BF16)
        vaug = jnp.where(hmask, v2, jnp.ones_like(v2))
        na.append(wprev * qc[:, hd * LANES:(hd + 1) * LANES] + _dot(p, vaug))
        m_end = mrow[end:end + 1]
        b_end = bcol[end:end + 1]
        wc = jnp.exp(m_h - m_end)
        ksc = (k2t * w[end:end + 1, :]).astype(BF16)
        upd = _dot(ksc, vaug)
        rmask = (sub < HEAD_DIM) if hd == 0 else (sub >= HEAD_DIM)
        c_ref[:, hd * LANES:(hd + 1) * LANES] = (
            wc * c_ref[:, hd * LANES:(hd + 1) * LANES] + jnp.where(rmask, upd, 0.0))
        m_new.append(b_end + m_end)
    num = jnp.where(head_a, na[0], na[1])
    den = pltpu.roll(jnp.where(head_a, na[1], na[0]), HEAD_DIM, axis=1)
    en = jnp.where(head_a, eneg[0], eneg[1])
    hacc_ref[rows, :] += num / jnp.maximum(jnp.abs(den), en)
    return m_new


def _mlstm_finish(hacc_ref, o_ref, g_ref, y_ref, n_chunks):
    L = MLSTM_L
    r_ = lax.broadcasted_iota(jnp.int32, (LANES, LANES), 0) // HEAD_DIM
    c_ = lax.broadcasted_iota(jnp.int32, (LANES, LANES), 1) // HEAD_DIM
    avg = jnp.where(r_ == c_, 1.0 / HEAD_DIM, 0.0).astype(BF16)

    def body(j, carry):
        rows = _chunk_rows(j)
        hs = hacc_ref[rows, :]
        mu = _split_dot(hs, avg, 2)
        dv = hs - mu
        var = _split_dot(dv * dv, avg, 2)
        gate = _sigmoid(o_ref[rows, :].astype(F32))
        y_ref[rows, :] = ((dv * lax.rsqrt(var + EPS)) * g_ref[...] * gate).astype(y_ref.dtype)
        return carry

    lax.fori_loop(0, n_chunks, body, 0)


def _mlstm_kernel(qx_ref, kx_ref, vx_ref, ox_ref, grx_ref, qc_ref, kc_ref, vc_ref, oc_ref, grc_ref,
                  g_ref, *rest, need_ctx, nh):
    if need_ctx:
        y_ref, yc_ref, cf_ref, cb_ref, hx_ref, hc_ref = rest
    else:
        y_ref, cf_ref, cb_ref, hx_ref, hc_ref = rest
        yc_ref = None
    L = MLSTM_L
    hp = pl.program_id(1)
    ncx = qx_ref.shape[0] // L
    ncc = qc_ref.shape[0] // L
    cf_ref[...] = jnp.zeros_like(cf_ref)
    cb_ref[...] = jnp.zeros_like(cb_ref)
    hx_ref[...] = jnp.zeros_like(hx_ref)
    hc_ref[...] = jnp.zeros_like(hc_ref)
    zero = jnp.zeros((1, 1), F32)
    mf = [zero, zero]
    mb = [zero, zero]
    step = functools.partial(_mlstm_chunk, nh=nh)
    for j in range(ncc):
        mf = step(qc_ref, kc_ref, vc_ref, grc_ref, j, hp, cf_ref, mf, hc_ref, rev=False)
        mb = step(qc_ref, kc_ref, vc_ref, grc_ref, ncc - 1 - j, hp, cb_ref, mb, hc_ref, rev=True)

    def body(j, carry):
        mf_, mb_ = list(carry[0:2]), list(carry[2:4])
        mf_ = step(qx_ref, kx_ref, vx_ref, grx_ref, j, hp, cf_ref, mf_, hx_ref, rev=False)
        mb_ = step(qx_ref, kx_ref, vx_ref, grx_ref, ncx - 1 - j, hp, cb_ref, mb_, hx_ref, rev=True)
        return (mf_[0], mf_[1], mb_[0], mb_[1])

    lax.fori_loop(0, ncx, body, (mf[0], mf[1], mb[0], mb[1]))
    _mlstm_finish(hx_ref, ox_ref, g_ref, y_ref, ncx)
    if need_ctx:
        _mlstm_finish(hc_ref, oc_ref, g_ref, yc_ref, ncc)


def _mlstm(px, grx, pc, grc, norm_g, *, need_ctx, d_grp):
    b, t, _ = px.shape
    ct = pc.shape[1]
    nh = d_grp // HEAD_DIM
    n_pairs = d_grp // LANES
    per = d_grp // LANES

    def col(k):
        return lambda bi, hp: (bi, 0, k * per + hp)

    def seq_specs(tlen, gr):
        return [pl.BlockSpec((None, tlen, LANES), col(k)) for k in range(4)] + [
            pl.BlockSpec((None,) + gr.shape[1:], lambda bi, hp: (bi, 0, 0, 0))]

    out_specs = [pl.BlockSpec((None, t, LANES), lambda bi, hp: (bi, 0, hp))]
    out_shape = [jax.ShapeDtypeStruct((b, t, d_grp), BF16)]
    if need_ctx:
        out_specs.append(pl.BlockSpec((None, ct, LANES), lambda bi, hp: (bi, 0, hp)))
        out_shape.append(jax.ShapeDtypeStruct((b, ct, d_grp), BF16))
    kern = functools.partial(_mlstm_kernel, need_ctx=need_ctx, nh=nh)
    res = pl.pallas_call(
        kern,
        grid=(b, n_pairs),
        in_specs=seq_specs(t, grx) + seq_specs(ct, grc) + [
            pl.BlockSpec((1, LANES), lambda bi, hp: (0, hp))],
        out_specs=out_specs,
        out_shape=out_shape,
        scratch_shapes=[
            pltpu.VMEM((LANES, 2 * LANES), F32),
            pltpu.VMEM((LANES, 2 * LANES), F32),
            pltpu.VMEM((t, LANES), F32),
            pltpu.VMEM((ct, LANES), F32),
        ],
        compiler_params=pltpu.CompilerParams(
            dimension_semantics=("arbitrary", "arbitrary"), vmem_limit_bytes=VMEM_LIMIT),
        name="mlstm_ctx" if need_ctx else "mlstm",
    )(px, px, px, px, grx, pc, pc, pc, pc, grc, norm_g)
    return (res[0], res[1]) if need_ctx else (res[0], None)


def _softmax_pv(scores, values):
    m = None
    for s in scores:
        ms = jnp.max(s, axis=1, keepdims=True)
        m = ms if m is None else jnp.maximum(m, ms)
    acc, den = None, None
    for s, v in zip(scores, values):
        p = jnp.exp(s - m)
        ds_ = jnp.sum(p, axis=1, keepdims=True)
        o = _dot(p.astype(BF16), v)
        acc = o if acc is None else acc + o
        den = ds_ if den is None else den + ds_
    return acc / den


def _na_kernel(q_ref, k_ref, v_ref, qc_ref, kc_ref, vc_ref, bt_ref, *rest, need_ctx, rows):
    if need_ctx:
        y_ref, yc_ref = rest
    else:
        (y_ref,) = rest
        yc_ref = None
    w = GRID_W
    qn = NA_QROWS * w
    kn = NA_WIN * w
    nblk = rows // NA_QROWS
    kc = kc_ref[...]
    vc = vc_ref[...]
    lane = lax.broadcasted_iota(jnp.int32, (qn, LANES), 1)
    head_a = lane < HEAD_DIM

    def body(ib, carry):
        r0 = ib * NA_QROWS
        u0 = jnp.clip(r0 - NA_ROWS // 2, 0, rows - NA_WIN)
        var = jnp.where(ib == 0, 0, jnp.where(ib == nblk - 1, 2, 1))
        qb = q_ref[pl.ds(pl.multiple_of(r0 * w, w), qn), :]
        kw = k_ref[pl.ds(pl.multiple_of(u0 * w, w), kn), :]
        vw = v_ref[pl.ds(pl.multiple_of(u0 * w, w), kn), :]
        outs = []
        for hd in range(2):
            hmask = head_a if hd == 0 else jnp.logical_not(head_a)
            qh = jnp.where(hmask, qb, jnp.zeros_like(qb))
            s_loc = _dot_nt(qh, kw) + bt_ref[var, hd]
            s_ctx = _dot_nt(qh, kc)
            outs.append(_softmax_pv([s_loc, s_ctx], [vw, vc]))
        y_ref[pl.ds(pl.multiple_of(r0 * w, w), qn), :] = (
            jnp.where(head_a, outs[0], outs[1]).astype(y_ref.dtype))
        return carry

    lax.fori_loop(0, nblk, body, 0)

    if need_ctx:
        qcb = qc_ref[...]
        lane_c = lax.broadcasted_iota(jnp.int32, qcb.shape, 1)
        head_ac = lane_c < HEAD_DIM
        outs = []
        for hd in range(2):
            hmask = head_ac if hd == 0 else jnp.logical_not(head_ac)
            qh = jnp.where(hmask, qcb, jnp.zeros_like(qcb))
            outs.append(_softmax_pv([_dot_nt(qh, kc)], [vc]))
        yc_ref[...] = jnp.where(head_ac, outs[0], outs[1]).astype(yc_ref.dtype)


def _na_bias_table(rpb):
    w = GRID_W
    cols = np.arange(w)
    col_start = np.clip(cols - NA_COLS // 2, 0, w - NA_COLS)
    col_mask = (cols[None, :] >= col_start[:, None]) & (cols[None, :] < col_start[:, None] + NA_COLS)
    dc_idx = np.clip(cols[None, :] - cols[:, None] + NA_COLS - 1, 0, 2 * NA_COLS - 2)
    bm = jnp.where(col_mask[None, None], rpb.astype(F32)[:, :, dc_idx], NEG)
    i = np.arange(NA_QROWS)[:, None]
    j = np.arange(NA_WIN)[None, :]
    offs = (0, NA_ROWS // 2, NA_ROWS)
    lo = (np.zeros_like(i), i, np.full_like(i, NA_QROWS))
    dr = np.stack([j - i + (NA_ROWS - 1) - o for o in offs])
    valid = np.stack([(j >= l) & (j < l + NA_ROWS) for l in lo])
    tiles = bm[:, np.clip(dr, 0, 2 * NA_ROWS - 2)]
    tiles = jnp.where(valid[None, :, :, :, None, None], tiles, NEG)
    h = rpb.shape[0]
    return tiles.transpose(1, 0, 2, 4, 3, 5).reshape(3, h, NA_QROWS * w, NA_WIN * w)


def _na(px, pc, btab, *, need_ctx, d_grp, col0):
    b, t, _ = px.shape
    ct = pc.shape[1]
    n_pairs = d_grp // LANES
    per = d_grp // LANES
    rows = t // GRID_W

    def col(k):
        return lambda bi, hp: (bi, 0, col0 + k * per + hp)

    in_specs = ([pl.BlockSpec((None, t, LANES), col(k)) for k in range(3)]
                + [pl.BlockSpec((None, ct, LANES), col(k)) for k in range(3)]
                + [pl.BlockSpec((3, 2) + btab.shape[2:], lambda bi, hp: (0, hp, 0, 0))])
    out_specs = [pl.BlockSpec((None, t, LANES), lambda bi, hp: (bi, 0, hp))]
    out_shape = [jax.ShapeDtypeStruct((b, t, d_grp), BF16)]
    if need_ctx:
        out_specs.append(pl.BlockSpec((None, ct, LANES), lambda bi, hp: (bi, 0, hp)))
        out_shape.append(jax.ShapeDtypeStruct((b, ct, d_grp), BF16))
    kern = functools.partial(_na_kernel, need_ctx=need_ctx, rows=rows)
    res = pl.pallas_call(
        kern,
        grid=(b, n_pairs),
        in_specs=in_specs,
        out_specs=out_specs,
        out_shape=out_shape,
        compiler_params=pltpu.CompilerParams(
            dimension_semantics=("arbitrary", "arbitrary"), vmem_limit_bytes=VMEM_LIMIT),
        name="natten_ctx" if need_ctx else "natten",
    )(px, px, px, pc, pc, pc, btab)
    return (res[0], res[1]) if need_ctx else (res[0], None)


def _out_mlp_kernel(x_ref, ym_ref, yn_ref, mod_ref, g_ref, wo_ref, w1_ref, w2_ref, fg_ref, o_ref,
                    *, d_model, final_norm):
    d = d_model
    dm = ym_ref.shape[1]
    att = _dot(ym_ref[...], wo_ref[0:dm, :]) + _dot(yn_ref[...], wo_ref[dm:, :])
    x1 = x_ref[...] + mod_ref[:, 2 * d:3 * d] * att
    h = _modulated_norm(x1, g_ref[...], mod_ref[:, 4 * d:5 * d], mod_ref[:, 3 * d:4 * d]).astype(BF16)
    d_ff = w1_ref.shape[1]
    acc = None
    for c in range(d_ff // COL_GROUP):
        hid = _dot(h, w1_ref[:, c * COL_GROUP:(c + 1) * COL_GROUP])
        hid = jnp.square(jnp.maximum(hid, 0.0)).astype(BF16)
        part = _dot(hid, w2_ref[c * COL_GROUP:(c + 1) * COL_GROUP, :])
        acc = part if acc is None else acc + part
    x2 = x1 + mod_ref[:, 5 * d:6 * d] * acc
    if final_norm:
        x2 = (x2 * lax.rsqrt(jnp.mean(x2 * x2, axis=-1, keepdims=True) + EPS)) * fg_ref[...]
    o_ref[...] = x2


def _out_mlp(x, ym, yn, mod4, mod_row, g, w_out, w1, w2, final_g, *, final_norm):
    b, t, d = x.shape
    dm = ym.shape[2]
    d_ff = w1.shape[1]
    tm = min(512, t)
    kern = functools.partial(_out_mlp_kernel, d_model=d, final_norm=final_norm)
    const = lambda i, bi: (0, 0)
    return pl.pallas_call(
        kern,
        grid=(t // tm, b),
        in_specs=[
            pl.BlockSpec((None, tm, d), lambda i, bi: (bi, i, 0)),
            pl.BlockSpec((None, tm, dm), lambda i, bi: (bi, i, 0)),
            pl.BlockSpec((None, tm, dm), lambda i, bi: (bi, i, 0)),
            pl.BlockSpec((None, 1, N_MOD * d), lambda i, bi: (mod_row(bi), 0, 0)),
            pl.BlockSpec((1, d), const),
            pl.BlockSpec((d, d), const, pipeline_mode=pl.Buffered(1)),
            pl.BlockSpec((d, d_ff), const, pipeline_mode=pl.Buffered(1)),
            pl.BlockSpec((d_ff, d), const, pipeline_mode=pl.Buffered(1)),
            pl.BlockSpec((1, d), const),
        ],
        out_specs=pl.BlockSpec((None, tm, d), lambda i, bi: (bi, i, 0)),
        out_shape=jax.ShapeDtypeStruct((b, t, d), F32),
        compiler_params=pltpu.CompilerParams(
            dimension_semantics=("arbitrary", "arbitrary"), vmem_limit_bytes=VMEM_LIMIT),
        name="out_mlp_final" if final_norm else "out_mlp",
    )(x, ym, yn, mod4, g, w_out, w1, w2, final_g)


def _rope_tables(n_tokens):
    t = jnp.arange(n_tokens)
    row = (t // GRID_W).astype(F32)
    col = (t % GRID_W).astype(F32)
    n_freq = HEAD_DIM // 4
    inv_freq = ROPE_BASE ** (-jnp.arange(n_freq, dtype=F32) / n_freq)
    ang = jnp.concatenate([row[:, None] * inv_freq, col[:, None] * inv_freq], axis=-1)
    cos, sin = jnp.cos(ang), jnp.sin(ang)
    reps = LANES // HEAD_DIM
    return (jnp.tile(jnp.concatenate([cos, cos], axis=-1), (1, reps)),
            jnp.tile(jnp.concatenate([-sin, sin], axis=-1), (1, reps)))


def kernel(x, c, ctx, c_ctx, w_ada, b_ada, norm1_g, w_in, b_gate, mlstm_norm_g, rpb, w_out, norm2_g,
           w_mlp1, w_mlp2, final_g):
    b, s, d = x.shape
    ct = ctx.shape[1]
    depth = w_ada.shape[0]
    d_grp = d // 2
    n_gate = b_gate.shape[1]
    n_main = w_in.shape[2] - n_gate
    assert n_main == 7 * d_grp and d_grp % LANES == 0 and n_gate == 4 * (d_grp // HEAD_DIM)
    assert s % 512 == 0 and ct % MLSTM_L == 0 and (s // GRID_W) % NA_QROWS == 0

    n_rows = -(-(b + 1) // 8) * 8
    cc = jnp.zeros((n_rows, d), F32).at[:b].set(c).at[b].set(c_ctx)
    mod = _adaln(cc, w_ada, b_ada)
    mod = mod.reshape(depth, n_rows, 1, N_MOD * d)

    cos_t, sin_t = _rope_tables(s)
    w_main = w_in[:, :, :n_main].astype(BF16)
    w_gate_t = jnp.swapaxes(w_in[:, :, n_main:], 1, 2).astype(BF16)
    b_gate_col = jnp.broadcast_to(b_gate.astype(F32)[:, :, None], (depth, n_gate, LANES))
    w_out_b = w_out.astype(BF16)
    w1_b = w_mlp1.astype(BF16)
    w2_b = w_mlp2.astype(BF16)

    x_row = lambda bi: bi
    c_row = lambda bi: b
    xc = ctx
    for l in range(depth):
        need_ctx = l < depth - 1
        g1 = norm1_g[l].reshape(1, d)
        g2 = norm2_g[l].reshape(1, d)
        px, grx = _in_proj(x, mod[l], x_row, g1, w_main[l], w_gate_t[l], b_gate_col[l],
                           cos_t, sin_t, rope=True)
        pc, grc = _in_proj(xc, mod[l], c_row, g1, w_main[l], w_gate_t[l], b_gate_col[l],
                           cos_t[:ct], sin_t[:ct], rope=False)
        ym, ycm = _mlstm(px, grx, pc, grc, mlstm_norm_g[l].reshape(1, d_grp),
                         need_ctx=need_ctx, d_grp=d_grp)
        btab = _na_bias_table(rpb[l])
        yn, ycn = _na(px, pc, btab, need_ctx=need_ctx, d_grp=d_grp, col0=4 * (d_grp // LANES))
        fg = final_g.reshape(1, d)
        x = _out_mlp(x, ym, yn, mod[l], x_row, g2, w_out_b[l], w1_b[l], w2_b[l], fg,
                     final_norm=(l == depth - 1))
        if need_ctx:
            xc = _out_mlp(xc, ycm, ycn, mod[l], c_row, g2, w_out_b[l], w1_b[l], w2_b[l], fg,
                          final_norm=False)
    return x
```

```python
import functools

import jax
import jax.numpy as jnp
import numpy as np
from jax import lax
from jax.experimental import pallas as pl
from jax.experimental.pallas import tpu as pltpu

HEAD_DIM = 64
GRID_W = 64
NA_ROWS = 8
NA_COLS = 16
N_MOD = 6
ROPE_BASE = 10000.0
EPS = 1e-6

LANES = 128
MLSTM_L = 128
NA_QROWS = 4
NA_WIN = NA_QROWS + NA_ROWS
NEG = -1e30
COL_GROUP = 512
TOKEN_TILE = 512
VMEM_LIMIT = 56 * 1024 * 1024

F32 = jnp.float32
BF16 = jnp.bfloat16


def _dot(a, b):
    return jnp.dot(a, b, preferred_element_type=F32)


def _dot_nt(a, b):
    return lax.dot_general(a, b, (((1,), (1,)), ((), ())), preferred_element_type=F32)


def _split3(a):
    p0 = a.astype(BF16)
    r1 = a - p0.astype(F32)
    p1 = r1.astype(BF16)
    p2 = (r1 - p1.astype(F32)).astype(BF16)
    return p0, p1, p2


def _sigmoid(v):
    return 1.0 / (1.0 + jnp.exp(-v))


def _log_sigmoid(v):
    return jnp.minimum(v, 0.0) - jnp.log(1.0 + jnp.exp(-jnp.abs(v)))


def _adaln_kernel(c_ref, w_ref, b_ref, o_ref):
    c = c_ref[...]
    s = (c * _sigmoid(c)).astype(BF16)
    o_ref[...] = _dot(s, w_ref[...].astype(BF16)) + b_ref[...]


def _adaln(cc, w_ada, b_ada):
    depth, d, n = w_ada.shape
    rows = cc.shape[0]
    tn = n // 4
    return pl.pallas_call(
        _adaln_kernel,
        grid=(depth, n // tn),
        in_specs=[
            pl.BlockSpec((rows, d), lambda l, j: (0, 0)),
            pl.BlockSpec((None, d, tn), lambda l, j: (l, 0, j)),
            pl.BlockSpec((None, 1, tn), lambda l, j: (l, 0, j)),
        ],
        out_specs=pl.BlockSpec((None, rows, tn), lambda l, j: (l, 0, j)),
        out_shape=jax.ShapeDtypeStruct((depth, rows, n), F32),
        compiler_params=pltpu.CompilerParams(
            dimension_semantics=("arbitrary", "arbitrary"), vmem_limit_bytes=VMEM_LIMIT),
        name="adaln",
    )(cc, w_ada, b_ada.reshape(depth, 1, n))


def _modulated_norm(x, g, scale, shift):
    y = x * lax.rsqrt(jnp.mean(x * x, axis=-1, keepdims=True) + EPS)
    return (y * g) * (1.0 + scale) + shift


def _in_proj_kernel(x_ref, mod_ref, g_ref, wt_ref, wf_ref, wg_ref, bg_ref, cos_ref, sin_ref,
                    pt_ref, pf_ref, gc_ref, gr_ref, *, d_model, rope, nh):
    d = d_model
    tm = x_ref.shape[0]
    L = MLSTM_L
    n_chunks = tm // L
    h = _modulated_norm(x_ref[...], g_ref[...], mod_ref[:, d:2 * d], mod_ref[:, 0:d])
    hb = h.astype(BF16)
    q_scale = HEAD_DIM ** -0.5
    half = HEAD_DIM // 2

    lane = lax.broadcasted_iota(jnp.int32, (tm, LANES), 1)
    first_half = (lane % HEAD_DIM) < half
    for cg in range(wt_ref.shape[1] // COL_GROUP):
        acc = _dot(hb, wt_ref[:, cg * COL_GROUP:(cg + 1) * COL_GROUP])
        for s in range(COL_GROUP // LANES):
            slab = acc[:, s * LANES:(s + 1) * LANES]
            if rope and cg == 0:
                partner = jnp.where(first_half,
                                    pltpu.roll(slab, LANES - half, axis=1),
                                    pltpu.roll(slab, half, axis=1))
                slab = slab * cos_ref[...] + partner * sin_ref[...]
            if cg == 2:
                slab = slab * q_scale
            c0 = cg * COL_GROUP + s * LANES
            pt_ref[:, c0:c0 + LANES] = slab.astype(BF16)

    d_grp = wf_ref.shape[0] // 2
    if rope:
        cos_t = cos_ref[...].T[0:half]
        sin_t = sin_ref[...].T[half:HEAD_DIM]
    for rg in range(wf_ref.shape[0] // LANES):
        acc = _dot_nt(wf_ref[rg * LANES:(rg + 1) * LANES, :], hb)
        is_q = rg * LANES < d_grp
        if is_q:
            if rope:
                parts = []
                for hh in range(LANES // HEAD_DIM):
                    x1 = acc[hh * HEAD_DIM:hh * HEAD_DIM + half]
                    x2 = acc[hh * HEAD_DIM + half:(hh + 1) * HEAD_DIM]
                    parts += [x1 * cos_t - x2 * sin_t, x1 * sin_t + x2 * cos_t]
                acc = jnp.concatenate(parts, axis=0)
            acc = acc * q_scale
        accb = acc.astype(BF16)
        for c in range(n_chunks):
            pf_ref[c, rg * LANES:(rg + 1) * LANES, :] = accb[:, c * L:(c + 1) * L]

    gt = _dot(hb, wg_ref[...]) + bg_ref[...]
    lf = _log_sigmoid(gt)
    t_ = lax.broadcasted_iota(jnp.int32, (L, L), 0)
    u_ = lax.broadcasted_iota(jnp.int32, (L, L), 1)
    tri_prefix = (u_ <= t_).astype(BF16)
    tri_suffix = (u_ >= t_).astype(BF16)
    lane_c = lax.broadcasted_iota(jnp.int32, (L, LANES), 1)
    for c in range(n_chunks):
        pieces = _split3(lf[c * L:(c + 1) * L])
        b_pre = sum(_dot(tri_prefix, p) for p in pieces)
        b_suf = sum(_dot(tri_suffix, p) for p in pieces)
        b_sel = jnp.where(lane_c < 3 * nh, b_pre, b_suf)
        a = gt[c * L:(c + 1) * L] - pltpu.roll(b_sel, LANES - 2 * nh, axis=1)
        tot = jnp.broadcast_to(jnp.sum(lf[c * L:(c + 1) * L], axis=0, keepdims=True), (L, LANES))
        amax = jnp.broadcast_to(jnp.max(a, axis=0, keepdims=True), (L, LANES))
        out = jnp.where(lane_c < 2 * nh, a,
              jnp.where(lane_c < 4 * nh, b_sel,
              jnp.where(lane_c < 6 * nh, pltpu.roll(tot, 2 * nh, axis=1),
                        pltpu.roll(amax, 6 * nh, axis=1))))
        gc_ref[c * L:(c + 1) * L, :] = out
        gr_ref[c] = out.T[0:8 * nh]


def _in_proj(x, mod4, mod_row, g, w_tok, w_feat, w_gate, b_gate, cos_t, sin_t, *, rope, nh):
    b, t, d = x.shape
    nt = w_tok.shape[1]
    nf = w_feat.shape[0]
    tm = min(TOKEN_TILE, t)
    nc = tm // MLSTM_L
    kern = functools.partial(_in_proj_kernel, d_model=d, rope=rope, nh=nh)
    const = lambda i, bi: (0, 0)
    return pl.pallas_call(
        kern,
        grid=(t // tm, b),
        in_specs=[
            pl.BlockSpec((None, tm, d), lambda i, bi: (bi, i, 0)),
            pl.BlockSpec((None, 1, N_MOD * d), lambda i, bi: (mod_row(bi), 0, 0)),
            pl.BlockSpec((1, d), const),
            pl.BlockSpec((d, nt), const, pipeline_mode=pl.Buffered(1)),
            pl.BlockSpec((nf, d), const, pipeline_mode=pl.Buffered(1)),
            pl.BlockSpec((d, LANES), const),
            pl.BlockSpec((1, LANES), const),
            pl.BlockSpec((tm, LANES), lambda i, bi: (i, 0)),
            pl.BlockSpec((tm, LANES), lambda i, bi: (i, 0)),
        ],
        out_specs=[
            pl.BlockSpec((None, tm, nt), lambda i, bi: (bi, i, 0)),
            pl.BlockSpec((None, nc, nf, MLSTM_L), lambda i, bi: (bi, i, 0, 0)),
            pl.BlockSpec((None, tm, LANES), lambda i, bi: (bi, i, 0)),
            pl.BlockSpec((None, nc, 8 * nh, MLSTM_L), lambda i, bi: (bi, i, 0, 0)),
        ],
        out_shape=[
            jax.ShapeDtypeStruct((b, t, nt), BF16),
            jax.ShapeDtypeStruct((b, t // MLSTM_L, nf, MLSTM_L), BF16),
            jax.ShapeDtypeStruct((b, t, LANES), F32),
            jax.ShapeDtypeStruct((b, t // MLSTM_L, 8 * nh, MLSTM_L), F32),
        ],
        compiler_params=pltpu.CompilerParams(
            dimension_semantics=("arbitrary", "arbitrary"), vmem_limit_bytes=VMEM_LIMIT),
        name="in_proj_rope" if rope else "in_proj",
    )(x, mod4, g, w_tok, w_feat, w_gate, b_gate, cos_t, sin_t)


def _chunk_rows(j):
    if isinstance(j, int):
        return pl.ds(j * MLSTM_L, MLSTM_L)
    return pl.ds(pl.multiple_of(j * MLSTM_L, MLSTM_L), MLSTM_L)


N_REP = 16
AUG = HEAD_DIM + N_REP


def _mlstm_intra(qt_ref, k_ref, gc_ref, gr_ref, j, tau, hp, m_state, p_ref, rv_ref, *, rev, nh):
    L = MLSTM_L
    d = HEAD_DIM
    rows = _chunk_rows(j)
    gcb = gc_ref[rows, :]
    qt = qt_ref[j]
    k2 = k_ref[rows, :]
    lane = lax.broadcasted_iota(jnp.int32, (L, LANES), 1)
    sub = lax.broadcasted_iota(jnp.int32, (L, LANES), 0)
    tri = (sub >= lane) if rev else (sub <= lane)
    zero = jnp.zeros_like(qt)
    qbd = jnp.concatenate([jnp.where(sub < d, qt, zero), jnp.where(sub >= d, qt, zero)], axis=1)
    st = _dot(k2, qbd)
    base = nh if rev else 0
    m_new = []
    for hd in range(2):
        r = base + 2 * hp + hd
        a_col = jnp.sum(jnp.where(lane == r, gcb, 0.0), axis=1, keepdims=True)
        a_row = gr_ref[j, pl.ds(r, 1), :]
        b_row = gr_ref[j, pl.ds(2 * nh + r, 1), :]
        tot = gr_ref[j, pl.ds(4 * nh + r, 1), :]
        amax = gr_ref[j, pl.ds(6 * nh + r, 1), :]
        m_h = m_state[hd]
        at = jnp.where(tri, a_col, NEG)
        mrow = jnp.maximum(jnp.max(at, axis=0, keepdims=True), m_h)
        p_ref[int(rev), tau, :, hd * L:(hd + 1) * L] = (
            st[:, hd * L:(hd + 1) * L] * jnp.exp(at - mrow)).astype(BF16)
        m_end = jnp.maximum(m_h, amax)
        rv_ref[int(rev), tau, 4 * hd:4 * hd + 4, :] = jnp.concatenate([
            jnp.exp(m_h - mrow),
            jnp.exp(-(b_row + mrow)),
            jnp.exp(a_row - m_end),
            jnp.exp(m_h - m_end),
        ], axis=0)
        m_new.append(tot + m_end)
    return m_new


def _mlstm_chunk(qt_ref, vt_ref, k_ref, j, tau, ct_ref, p_ref, rv_ref, hacc_ref, *, rev):
    L = MLSTM_L
    d = HEAD_DIM
    qt = qt_ref[j]
    vt = vt_ref[j]
    k2 = k_ref[_chunk_rows(j), :]
    rv = rv_ref[int(rev), tau]
    ones = jnp.ones((N_REP, L), BF16)
    pv = _dot(jnp.concatenate([vt, ones], axis=0), p_ref[int(rev), tau])
    qc = _dot(ct_ref[...].astype(BF16), qt)
    hs, vsc = [], []
    for hd in range(2):
        wprev, eneg, ws = rv[4 * hd:4 * hd + 1], rv[4 * hd + 1:4 * hd + 2], rv[4 * hd + 2:4 * hd + 3]
        cols = slice(hd * L, (hd + 1) * L)
        num = wprev * qc[hd * AUG:hd * AUG + d] + pv[hd * d:(hd + 1) * d, cols]
        den = wprev * qc[hd * AUG + d:hd * AUG + d + 1] + pv[2 * d:2 * d + 1, cols]
        hs.append(num * (1.0 / jnp.maximum(jnp.abs(den), eneg)))
        vh = vt[hd * d:(hd + 1) * d].astype(F32) * ws
        vsc += [vh.astype(BF16), jnp.broadcast_to(ws, (N_REP, L)).astype(BF16)]
    hacc_ref[int(rev), j] = jnp.concatenate(hs, axis=0)
    upd = _dot(jnp.concatenate(vsc, axis=0), k2)
    lane = lax.broadcasted_iota(jnp.int32, (AUG, LANES), 1)
    for hd in range(2):
        wc = rv[4 * hd + 3:4 * hd + 4]
        own = (lane < d) if hd == 0 else (lane >= d)
        blk = slice(hd * AUG, (hd + 1) * AUG)
        ct_ref[blk, :] = wc * ct_ref[blk, :] + jnp.where(own, upd[blk], 0.0)


def _mlstm_finish(hacc_ref, o_ref, g_ref, y_ref, n_chunks):
    sub = lax.broadcasted_iota(jnp.int32, (LANES, MLSTM_L), 0)
    row_a = sub < HEAD_DIM

    def head_mean(v):
        ma = jnp.mean(v[:HEAD_DIM], axis=0, keepdims=True)
        mb = jnp.mean(v[HEAD_DIM:], axis=0, keepdims=True)
        return jnp.where(row_a, ma, mb)

    def body(j, carry):
        rows = _chunk_rows(j)
        hs = hacc_ref[0, j] + hacc_ref[1, j]
        dv = hs - head_mean(hs)
        yn = (dv * lax.rsqrt(head_mean(dv * dv) + EPS)).T
        gate = _sigmoid(o_ref[rows, :].astype(F32))
        y_ref[rows, :] = (yn * g_ref[...] * gate).astype(y_ref.dtype)
        return carry

    lax.fori_loop(0, n_chunks, body, 0, unroll=2)


def _mlstm_kernel(qtx_ref, vtx_ref, kx_ref, ox_ref, gcx_ref, grx_ref,
                  qtc_ref, vtc_ref, kc_ref, oc_ref, gcc_ref, grc_ref, g_ref, *rest, need_ctx, nh):
    if need_ctx:
        y_ref, yc_ref, cf_ref, cb_ref, p_ref, rv_ref, hx_ref, hc_ref = rest
    else:
        y_ref, cf_ref, cb_ref, p_ref, rv_ref, hx_ref, hc_ref = rest
        yc_ref = None
    hp = pl.program_id(1)
    ncx = qtx_ref.shape[0]
    ncc = qtc_ref.shape[0]
    intra = functools.partial(_mlstm_intra, hp=hp, p_ref=p_ref, rv_ref=rv_ref, nh=nh)
    zero = jnp.zeros((1, MLSTM_L), F32)
    mf = [zero, zero]
    mb = [zero, zero]
    for j in range(ncc):
        mf = intra(qtc_ref, kc_ref, gcc_ref, grc_ref, j, j, m_state=mf, rev=False)
        mb = intra(qtc_ref, kc_ref, gcc_ref, grc_ref, ncc - 1 - j, j, m_state=mb, rev=True)

    def intra_body(j, carry):
        mf_ = intra(qtx_ref, kx_ref, gcx_ref, grx_ref, j, ncc + j, m_state=list(carry[0:2]), rev=False)
        mb_ = intra(qtx_ref, kx_ref, gcx_ref, grx_ref, ncx - 1 - j, ncc + j, m_state=list(carry[2:4]),
                    rev=True)
        return (mf_[0], mf_[1], mb_[0], mb_[1])

    lax.fori_loop(0, ncx, intra_body, (mf[0], mf[1], mb[0], mb[1]))

    cf_ref[...] = jnp.zeros_like(cf_ref)
    cb_ref[...] = jnp.zeros_like(cb_ref)
    step = functools.partial(_mlstm_chunk, p_ref=p_ref, rv_ref=rv_ref)
    for j in range(ncc):
        step(qtc_ref, vtc_ref, kc_ref, j, j, cf_ref, hacc_ref=hc_ref, rev=False)
        step(qtc_ref, vtc_ref, kc_ref, ncc - 1 - j, j, cb_ref, hacc_ref=hc_ref, rev=True)

    def body(j, carry):
        step(qtx_ref, vtx_ref, kx_ref, j, ncc + j, cf_ref, hacc_ref=hx_ref, rev=False)
        step(qtx_ref, vtx_ref, kx_ref, ncx - 1 - j, ncc + j, cb_ref, hacc_ref=hx_ref, rev=True)
        return carry

    lax.fori_loop(0, ncx, body, 0)
    _mlstm_finish(hx_ref, ox_ref, g_ref, y_ref, ncx)
    if need_ctx:
        _mlstm_finish(hc_ref, oc_ref, g_ref, yc_ref, ncc)


def _mlstm(ptx, pfx, gcx, grx, ptc, pfc, gcc, grc, norm_g, *, need_ctx, d_grp):
    b, t, _ = ptx.shape
    ct = ptc.shape[1]
    nh = d_grp // HEAD_DIM
    n_pairs = d_grp // LANES
    n_steps = (t + ct) // MLSTM_L

    def seq_specs(tlen, pf, gr):
        nc = tlen // MLSTM_L
        return [
            pl.BlockSpec((None, nc, LANES, MLSTM_L), lambda bi, hp: (bi, 0, hp, 0)),
            pl.BlockSpec((None, nc, LANES, MLSTM_L), lambda bi, hp: (bi, 0, n_pairs + hp, 0)),
            pl.BlockSpec((None, tlen, LANES), lambda bi, hp: (bi, 0, hp)),
            pl.BlockSpec((None, tlen, LANES), lambda bi, hp: (bi, 0, n_pairs + hp)),
            pl.BlockSpec((None, tlen, LANES), lambda bi, hp: (bi, 0, 0)),
            pl.BlockSpec((None,) + gr.shape[1:], lambda bi, hp: (bi, 0, 0, 0)),
        ]

    out_specs = [pl.BlockSpec((None, t, LANES), lambda bi, hp: (bi, 0, hp))]
    out_shape = [jax.ShapeDtypeStruct((b, t, d_grp), BF16)]
    if need_ctx:
        out_specs.append(pl.BlockSpec((None, ct, LANES), lambda bi, hp: (bi, 0, hp)))
        out_shape.append(jax.ShapeDtypeStruct((b, ct, d_grp), BF16))
    kern = functools.partial(_mlstm_kernel, need_ctx=need_ctx, nh=nh)
    res = pl.pallas_call(
        kern,
        grid=(b, n_pairs),
        in_specs=seq_specs(t, pfx, grx) + seq_specs(ct, pfc, grc) + [
            pl.BlockSpec((1, LANES), lambda bi, hp: (0, hp))],
        out_specs=out_specs,
        out_shape=out_shape,
        scratch_shapes=[
            pltpu.VMEM((2 * AUG, LANES), F32),
            pltpu.VMEM((2 * AUG, LANES), F32),
            pltpu.VMEM((2, n_steps, MLSTM_L, 2 * MLSTM_L), BF16),
            pltpu.VMEM((2, n_steps, 8, MLSTM_L), F32),
            pltpu.VMEM((2, t // MLSTM_L, LANES, MLSTM_L), F32),
            pltpu.VMEM((2, ct // MLSTM_L, LANES, MLSTM_L), F32),
        ],
        compiler_params=pltpu.CompilerParams(
            dimension_semantics=("arbitrary", "arbitrary"), vmem_limit_bytes=VMEM_LIMIT),
        name="mlstm_ctx" if need_ctx else "mlstm",
    )(pfx, pfx, ptx, ptx, gcx, grx, pfc, pfc, ptc, ptc, gcc, grc, norm_g)
    return (res[0], res[1]) if need_ctx else (res[0], None)


def _softmax_pv(scores, values):
    m = None
    for s in scores:
        ms = jnp.max(s, axis=1, keepdims=True)
        m = ms if m is None else jnp.maximum(m, ms)
    acc, den = None, None
    for s, v in zip(scores, values):
        p = jnp.exp(s - m)
        ds_ = jnp.sum(p, axis=1, keepdims=True)
        o = _dot(p.astype(BF16), v)
        acc = o if acc is None else acc + o
        den = ds_ if den is None else den + ds_
    return acc / den


def _na_kernel(q_ref, k_ref, v_ref, qc_ref, kc_ref, vc_ref, bt_ref, *rest, need_ctx, rows):
    if need_ctx:
        y_ref, yc_ref = rest
    else:
        (y_ref,) = rest
        yc_ref = None
    w = GRID_W
    qn = NA_QROWS * w
    kn = NA_WIN * w
    nblk = rows // NA_QROWS
    kc = kc_ref[...]
    vc = vc_ref[...]
    lane = lax.broadcasted_iota(jnp.int32, (qn, LANES), 1)
    head_a = lane < HEAD_DIM

    def body(ib, carry):
        r0 = ib * NA_QROWS
        u0 = jnp.clip(r0 - NA_ROWS // 2, 0, rows - NA_WIN)
        var = jnp.where(ib == 0, 0, jnp.where(ib == nblk - 1, 2, 1))
        qb = q_ref[pl.ds(pl.multiple_of(r0 * w, w), qn), :]
        kw = k_ref[pl.ds(pl.multiple_of(u0 * w, w), kn), :]
        vw = v_ref[pl.ds(pl.multiple_of(u0 * w, w), kn), :]
        outs = []
        for hd in range(2):
            hmask = head_a if hd == 0 else jnp.logical_not(head_a)
            qh = jnp.where(hmask, qb, jnp.zeros_like(qb))
            s_loc = _dot_nt(qh, kw) + bt_ref[var, hd]
            s_ctx = _dot_nt(qh, kc)
            outs.append(_softmax_pv([s_loc, s_ctx], [vw, vc]))
        y_ref[pl.ds(pl.multiple_of(r0 * w, w), qn), :] = (
            jnp.where(head_a, outs[0], outs[1]).astype(y_ref.dtype))
        return carry

    lax.fori_loop(0, nblk, body, 0)

    if need_ctx:
        qcb = qc_ref[...]
        lane_c = lax.broadcasted_iota(jnp.int32, qcb.shape, 1)
        head_ac = lane_c < HEAD_DIM
        outs = []
        for hd in range(2):
            hmask = head_ac if hd == 0 else jnp.logical_not(head_ac)
            qh = jnp.where(hmask, qcb, jnp.zeros_like(qcb))
            outs.append(_softmax_pv([_dot_nt(qh, kc)], [vc]))
        yc_ref[...] = jnp.where(head_ac, outs[0], outs[1]).astype(yc_ref.dtype)


def _na_bias_table(rpb):
    w = GRID_W
    cols = np.arange(w)
    col_start = np.clip(cols - NA_COLS // 2, 0, w - NA_COLS)
    col_mask = (cols[None, :] >= col_start[:, None]) & (cols[None, :] < col_start[:, None] + NA_COLS)
    dc_idx = np.clip(cols[None, :] - cols[:, None] + NA_COLS - 1, 0, 2 * NA_COLS - 2)
    bm = jnp.where(col_mask[None, None], rpb.astype(F32)[:, :, dc_idx], NEG)
    i = np.arange(NA_QROWS)[:, None]
    j = np.arange(NA_WIN)[None, :]
    offs = (0, NA_ROWS // 2, NA_ROWS)
    lo = (np.zeros_like(i), i, np.full_like(i, NA_QROWS))
    dr = np.stack([j - i + (NA_ROWS - 1) - o for o in offs])
    valid = np.stack([(j >= l) & (j < l + NA_ROWS) for l in lo])
    tiles = bm[:, np.clip(dr, 0, 2 * NA_ROWS - 2)]
    tiles = jnp.where(valid[None, :, :, :, None, None], tiles, NEG)
    h = rpb.shape[0]
    return tiles.transpose(1, 0, 2, 4, 3, 5).reshape(3, h, NA_QROWS * w, NA_WIN * w)


def _na(px, pc, btab, *, need_ctx, d_grp, col0):
    b, t, _ = px.shape
    ct = pc.shape[1]
    n_pairs = d_grp // LANES
    per = d_grp // LANES
    rows = t // GRID_W

    def col(k):
        return lambda bi, hp: (bi, 0, col0 + k * per + hp)

    in_specs = ([pl.BlockSpec((None, t, LANES), col(k)) for k in range(3)]
                + [pl.BlockSpec((None, ct, LANES), col(k)) for k in range(3)]
                + [pl.BlockSpec((3, 2) + btab.shape[2:], lambda bi, hp: (0, hp, 0, 0))])
    out_specs = [pl.BlockSpec((None, t, LANES), lambda bi, hp: (bi, 0, hp))]
    out_shape = [jax.ShapeDtypeStruct((b, t, d_grp), BF16)]
    if need_ctx:
        out_specs.append(pl.BlockSpec((None, ct, LANES), lambda bi, hp: (bi, 0, hp)))
        out_shape.append(jax.ShapeDtypeStruct((b, ct, d_grp), BF16))
    kern = functools.partial(_na_kernel, need_ctx=need_ctx, rows=rows)
    res = pl.pallas_call(
        kern,
        grid=(b, n_pairs),
        in_specs=in_specs,
        out_specs=out_specs,
        out_shape=out_shape,
        compiler_params=pltpu.CompilerParams(
            dimension_semantics=("arbitrary", "arbitrary"), vmem_limit_bytes=VMEM_LIMIT),
        name="natten_ctx" if need_ctx else "natten",
    )(px, px, px, pc, pc, pc, btab)
    return (res[0], res[1]) if need_ctx else (res[0], None)


def _out_mlp_kernel(x_ref, ym_ref, yn_ref, mod_ref, g_ref, wo_ref, w1_ref, w2_ref, fg_ref, o_ref,
                    *, d_model, final_norm):
    d = d_model
    dm = ym_ref.shape[1]
    att = _dot(ym_ref[...], wo_ref[0:dm, :]) + _dot(yn_ref[...], wo_ref[dm:, :])
    x1 = x_ref[...] + mod_ref[:, 2 * d:3 * d] * att
    h = _modulated_norm(x1, g_ref[...], mod_ref[:, 4 * d:5 * d], mod_ref[:, 3 * d:4 * d]).astype(BF16)
    d_ff = w1_ref.shape[1]
    acc = None
    for c in range(d_ff // COL_GROUP):
        hid = _dot(h, w1_ref[:, c * COL_GROUP:(c + 1) * COL_GROUP])
        hid = jnp.square(jnp.maximum(hid, 0.0)).astype(BF16)
        part = _dot(hid, w2_ref[c * COL_GROUP:(c + 1) * COL_GROUP, :])
        acc = part if acc is None else acc + part
    x2 = x1 + mod_ref[:, 5 * d:6 * d] * acc
    if final_norm:
        x2 = (x2 * lax.rsqrt(jnp.mean(x2 * x2, axis=-1, keepdims=True) + EPS)) * fg_ref[...]
    o_ref[...] = x2


def _out_mlp(x, ym, yn, mod4, mod_row, g, w_out, w1, w2, final_g, *, final_norm):
    b, t, d = x.shape
    dm = ym.shape[2]
    d_ff = w1.shape[1]
    tm = min(TOKEN_TILE, t)
    kern = functools.partial(_out_mlp_kernel, d_model=d, final_norm=final_norm)
    const = lambda i, bi: (0, 0)
    return pl.pallas_call(
        kern,
        grid=(t // tm, b),
        in_specs=[
            pl.BlockSpec((None, tm, d), lambda i, bi: (bi, i, 0)),
            pl.BlockSpec((None, tm, dm), lambda i, bi: (bi, i, 0)),
            pl.BlockSpec((None, tm, dm), lambda i, bi: (bi, i, 0)),
            pl.BlockSpec((None, 1, N_MOD * d), lambda i, bi: (mod_row(bi), 0, 0)),
            pl.BlockSpec((1, d), const),
            pl.BlockSpec((d, d), const, pipeline_mode=pl.Buffered(1)),
            pl.BlockSpec((d, d_ff), const, pipeline_mode=pl.Buffered(1)),
            pl.BlockSpec((d_ff, d), const, pipeline_mode=pl.Buffered(1)),
            pl.BlockSpec((1, d), const),
        ],
        out_specs=pl.BlockSpec((None, tm, d), lambda i, bi: (bi, i, 0)),
        out_shape=jax.ShapeDtypeStruct((b, t, d), F32),
        compiler_params=pltpu.CompilerParams(
            dimension_semantics=("arbitrary", "arbitrary"), vmem_limit_bytes=VMEM_LIMIT),
        name="out_mlp_final" if final_norm else "out_mlp",
    )(x, ym, yn, mod4, g, w_out, w1, w2, final_g)


def _rope_tables(n_tokens):
    t = jnp.arange(n_tokens)
    row = (t // GRID_W).astype(F32)
    col = (t % GRID_W).astype(F32)
    n_freq = HEAD_DIM // 4
    inv_freq = ROPE_BASE ** (-jnp.arange(n_freq, dtype=F32) / n_freq)
    ang = jnp.concatenate([row[:, None] * inv_freq, col[:, None] * inv_freq], axis=-1)
    cos, sin = jnp.cos(ang), jnp.sin(ang)
    reps = LANES // HEAD_DIM
    return (jnp.tile(jnp.concatenate([cos, cos], axis=-1), (1, reps)),
            jnp.tile(jnp.concatenate([-sin, sin], axis=-1), (1, reps)))


def kernel(x, c, ctx, c_ctx, w_ada, b_ada, norm1_g, w_in, b_gate, mlstm_norm_g, rpb, w_out, norm2_g,
           w_mlp1, w_mlp2, final_g):
    b, s, d = x.shape
    ct = ctx.shape[1]
    depth = w_ada.shape[0]
    d_grp = d // 2
    nh = d_grp // HEAD_DIM
    n_gate = b_gate.shape[1]
    n_main = w_in.shape[2] - n_gate
    assert n_main == 7 * d_grp and d_grp % LANES == 0 and n_gate == 4 * nh and n_gate <= LANES
    assert s % TOKEN_TILE == 0 and ct % MLSTM_L == 0 and (s // GRID_W) % NA_QROWS == 0

    n_rows = -(-(b + 1) // 8) * 8
    cc = jnp.zeros((n_rows, d), F32).at[:b].set(c).at[b].set(c_ctx)
    mod = _adaln(cc, w_ada, b_ada)
    mod = mod.reshape(depth, n_rows, 1, N_MOD * d)

    cos_t, sin_t = _rope_tables(s)
    grp = lambda k: w_in[:, :, k * d_grp:(k + 1) * d_grp]
    w_tok = jnp.concatenate([grp(1), grp(3), grp(4), grp(5), grp(6)], axis=2).astype(BF16)
    w_feat = jnp.swapaxes(jnp.concatenate([grp(0), grp(2)], axis=2), 1, 2).astype(BF16)
    w_gate = jnp.pad(w_in[:, :, n_main:], ((0, 0), (0, 0), (0, LANES - n_gate))).astype(BF16)
    b_gate_p = jnp.pad(b_gate.astype(F32), ((0, 0), (0, LANES - n_gate))).reshape(depth, 1, LANES)
    w_out_b = w_out.astype(BF16)
    w1_b = w_mlp1.astype(BF16)
    w2_b = w_mlp2.astype(BF16)
    na_col0 = 2 * (d_grp // LANES)

    x_row = lambda bi: bi
    c_row = lambda bi: b
    xc = ctx
    for l in range(depth):
        need_ctx = l < depth - 1
        g1 = norm1_g[l].reshape(1, d)
        g2 = norm2_g[l].reshape(1, d)
        ptx, pfx, gcx, grx = _in_proj(x, mod[l], x_row, g1, w_tok[l], w_feat[l], w_gate[l], b_gate_p[l],
                                      cos_t, sin_t, rope=True, nh=nh)
        ptc, pfc, gcc, grc = _in_proj(xc, mod[l], c_row, g1, w_tok[l], w_feat[l], w_gate[l], b_gate_p[l],
                                      cos_t[:ct], sin_t[:ct], rope=False, nh=nh)
        ym, ycm = _mlstm(ptx, pfx, gcx, grx, ptc, pfc, gcc, grc, mlstm_norm_g[l].reshape(1, d_grp),
                         need_ctx=need_ctx, d_grp=d_grp)
        btab = _na_bias_table(rpb[l])
        yn, ycn = _na(ptx, ptc, btab, need_ctx=need_ctx, d_grp=d_grp, col0=na_col0)
        fg = final_g.reshape(1, d)
        x = _out_mlp(x, ym, yn, mod[l], x_row, g2, w_out_b[l], w1_b[l], w2_b[l], fg,
                     final_norm=(l == depth - 1))
        if need_ctx:
            xc = _out_mlp(xc, ycm, ycn, mod[l], c_row, g2, w_out_b[l], w1_b[l], w2_b[l], fg,
                          final_norm=False)
    return x
```

```python
import functools

import jax
import jax.numpy as jnp
import numpy as np
from jax import lax
from jax.experimental import pallas as pl
from jax.experimental.pallas import tpu as pltpu

HEAD_DIM = 64
GRID_W = 64
NA_ROWS = 8
NA_COLS = 16
N_MOD = 6
ROPE_BASE = 10000.0
EPS = 1e-6

LANES = 128
MLSTM_L = 128
NA_QROWS = 4
NA_WIN = NA_QROWS + NA_ROWS
NEG = -1e30
LOG2E = 1.4426950408889634
COL_GROUP = 512
TOKEN_TILE = 512
VMEM_LIMIT = 56 * 1024 * 1024

F32 = jnp.float32
BF16 = jnp.bfloat16


def _dot(a, b):
    return jnp.dot(a, b, preferred_element_type=F32)


def _dot_nt(a, b):
    return lax.dot_general(a, b, (((1,), (1,)), ((), ())), preferred_element_type=F32)


def _split3(a):
    p0 = a.astype(BF16)
    r1 = a - p0.astype(F32)
    p1 = r1.astype(BF16)
    p2 = (r1 - p1.astype(F32)).astype(BF16)
    return p0, p1, p2


def _sigmoid(v):
    return 1.0 / (1.0 + jnp.exp(-v))


def _log_sigmoid(v):
    return jnp.minimum(v, 0.0) - jnp.log(1.0 + jnp.exp(-jnp.abs(v)))


def _adaln_kernel(c_ref, w_ref, b_ref, o_ref):
    c = c_ref[...]
    s = (c * _sigmoid(c)).astype(BF16)
    o_ref[...] = _dot(s, w_ref[...].astype(BF16)) + b_ref[...]


def _adaln(cc, w_ada, b_ada):
    depth, d, n = w_ada.shape
    rows = cc.shape[0]
    tn = n // 4
    return pl.pallas_call(
        _adaln_kernel,
        grid=(depth, n // tn),
        in_specs=[
            pl.BlockSpec((rows, d), lambda l, j: (0, 0)),
            pl.BlockSpec((None, d, tn), lambda l, j: (l, 0, j)),
            pl.BlockSpec((None, 1, tn), lambda l, j: (l, 0, j)),
        ],
        out_specs=pl.BlockSpec((None, rows, tn), lambda l, j: (l, 0, j)),
        out_shape=jax.ShapeDtypeStruct((depth, rows, n), F32),
        compiler_params=pltpu.CompilerParams(
            dimension_semantics=("arbitrary", "arbitrary"), vmem_limit_bytes=VMEM_LIMIT),
        name="adaln",
    )(cc, w_ada, b_ada.reshape(depth, 1, n))


def _modulated_norm(x, g, scale, shift):
    y = x * lax.rsqrt(jnp.mean(x * x, axis=-1, keepdims=True) + EPS)
    return (y * g) * (1.0 + scale) + shift


def _in_proj_kernel(x_ref, mod_ref, g_ref, wt_ref, wf_ref, wg_ref, bg_ref, cos_ref, sin_ref,
                    pt_ref, pf_ref, gc_ref, gr_ref, *, d_model, rope, nh):
    d = d_model
    tm = x_ref.shape[0]
    L = MLSTM_L
    n_chunks = tm // L
    h = _modulated_norm(x_ref[...], g_ref[...], mod_ref[:, d:2 * d], mod_ref[:, 0:d])
    hb = h.astype(BF16)
    q_scale = HEAD_DIM ** -0.5
    half = HEAD_DIM // 2

    gt = _dot(hb, wg_ref[...]) + bg_ref[...]
    lf = _log_sigmoid(gt)
    t_ = lax.broadcasted_iota(jnp.int32, (L, L), 0)
    u_ = lax.broadcasted_iota(jnp.int32, (L, L), 1)
    tri_prefix = (u_ <= t_).astype(BF16)
    tri_suffix = (u_ >= t_).astype(BF16)
    lane_c = lax.broadcasted_iota(jnp.int32, (L, LANES), 1)
    for c in range(n_chunks):
        pieces = _split3(lf[c * L:(c + 1) * L])
        b_pre = sum(_dot(tri_prefix, p) for p in pieces)
        b_suf = sum(_dot(tri_suffix, p) for p in pieces)
        b_sel = jnp.where(lane_c < 3 * nh, b_pre, b_suf)
        a = gt[c * L:(c + 1) * L] - pltpu.roll(b_sel, LANES - 2 * nh, axis=1)
        tot = jnp.broadcast_to(jnp.sum(lf[c * L:(c + 1) * L], axis=0, keepdims=True), (L, LANES))
        amax = jnp.broadcast_to(jnp.max(a, axis=0, keepdims=True), (L, LANES))
        out = jnp.where(lane_c < 2 * nh, a,
              jnp.where(lane_c < 4 * nh, b_sel,
              jnp.where(lane_c < 6 * nh, pltpu.roll(tot, 2 * nh, axis=1),
                        pltpu.roll(amax, 6 * nh, axis=1))))
        out = out * LOG2E
        gc_ref[c * L:(c + 1) * L, :] = out
        gr_ref[c] = out.T[0:8 * nh]

    lane = lax.broadcasted_iota(jnp.int32, (tm, LANES), 1)
    first_half = (lane % HEAD_DIM) < half
    for cg in range(wt_ref.shape[1] // COL_GROUP):
        acc = _dot(hb, wt_ref[:, cg * COL_GROUP:(cg + 1) * COL_GROUP])
        for s in range(COL_GROUP // LANES):
            slab = acc[:, s * LANES:(s + 1) * LANES]
            if rope and cg == 0:
                partner = jnp.where(first_half,
                                    pltpu.roll(slab, LANES - half, axis=1),
                                    pltpu.roll(slab, half, axis=1))
                slab = slab * cos_ref[...] + partner * sin_ref[...]
            if cg == 2:
                slab = slab * (q_scale * LOG2E)
            c0 = cg * COL_GROUP + s * LANES
            pt_ref[:, c0:c0 + LANES] = slab.astype(BF16)

    d_grp = wf_ref.shape[0] // 2
    if rope:
        cos_t = cos_ref[...].T[0:half]
        sin_t = sin_ref[...].T[half:HEAD_DIM]
    for part in range(2):
        acc = _dot_nt(wf_ref[part * d_grp:(part + 1) * d_grp, :], hb)
        for hh in range(d_grp // HEAD_DIM):
            r0 = hh * HEAD_DIM
            if part == 0 and rope:
                x1 = acc[r0:r0 + half]
                x2 = acc[r0 + half:r0 + HEAD_DIM]
                blk = jnp.concatenate([x1 * cos_t - x2 * sin_t, x1 * sin_t + x2 * cos_t], axis=0)
            else:
                blk = acc[r0:r0 + HEAD_DIM]
            if part == 0:
                blk = blk * q_scale
            blk = blk.astype(BF16)
            for c in range(n_chunks):
                pf_ref[c, part * d_grp + r0:part * d_grp + r0 + HEAD_DIM, :] = blk[:, c * L:(c + 1) * L]


def _in_proj(x, mod4, mod_row, g, w_tok, w_feat, w_gate, b_gate, cos_t, sin_t, *, rope, nh):
    b, t, d = x.shape
    nt = w_tok.shape[1]
    nf = w_feat.shape[0]
    tm = min(TOKEN_TILE, t)
    nc = tm // MLSTM_L
    kern = functools.partial(_in_proj_kernel, d_model=d, rope=rope, nh=nh)
    const = lambda i, bi: (0, 0)
    return pl.pallas_call(
        kern,
        grid=(t // tm, b),
        in_specs=[
            pl.BlockSpec((None, tm, d), lambda i, bi: (bi, i, 0)),
            pl.BlockSpec((None, 1, N_MOD * d), lambda i, bi: (mod_row(bi), 0, 0)),
            pl.BlockSpec((1, d), const),
            pl.BlockSpec((d, nt), const, pipeline_mode=pl.Buffered(1)),
            pl.BlockSpec((nf, d), const, pipeline_mode=pl.Buffered(1)),
            pl.BlockSpec((d, LANES), const),
            pl.BlockSpec((1, LANES), const),
            pl.BlockSpec((tm, LANES), lambda i, bi: (i, 0)),
            pl.BlockSpec((tm, LANES), lambda i, bi: (i, 0)),
        ],
        out_specs=[
            pl.BlockSpec((None, tm, nt), lambda i, bi: (bi, i, 0)),
            pl.BlockSpec((None, nc, nf, MLSTM_L), lambda i, bi: (bi, i, 0, 0)),
            pl.BlockSpec((None, tm, LANES), lambda i, bi: (bi, i, 0)),
            pl.BlockSpec((None, nc, 8 * nh, MLSTM_L), lambda i, bi: (bi, i, 0, 0)),
        ],
        out_shape=[
            jax.ShapeDtypeStruct((b, t, nt), BF16),
            jax.ShapeDtypeStruct((b, t // MLSTM_L, nf, MLSTM_L), BF16),
            jax.ShapeDtypeStruct((b, t, LANES), F32),
            jax.ShapeDtypeStruct((b, t // MLSTM_L, 8 * nh, MLSTM_L), F32),
        ],
        compiler_params=pltpu.CompilerParams(
            dimension_semantics=("arbitrary", "arbitrary"), vmem_limit_bytes=VMEM_LIMIT),
        name="in_proj_rope" if rope else "in_proj",
    )(x, mod4, g, w_tok, w_feat, w_gate, b_gate, cos_t, sin_t)


def _chunk_rows(j):
    if isinstance(j, int):
        return pl.ds(j * MLSTM_L, MLSTM_L)
    return pl.ds(pl.multiple_of(j * MLSTM_L, MLSTM_L), MLSTM_L)


N_REP = 16
AUG = HEAD_DIM + N_REP


def _mlstm_intra(qt_ref, vt_ref, k_ref, gc_ref, gr_ref, j, tau, hp, m_state, p_ref, rv_ref, u_ref,
                 *, rev, nh):
    L = MLSTM_L
    d = HEAD_DIM
    rows = _chunk_rows(j)
    gcb = gc_ref[rows, :]
    qt = qt_ref[j]
    k2 = k_ref[rows, :]
    lane = lax.broadcasted_iota(jnp.int32, (L, LANES), 1)
    sub = lax.broadcasted_iota(jnp.int32, (L, LANES), 0)
    tri = (sub >= lane) if rev else (sub <= lane)
    zero = jnp.zeros_like(qt)
    qbd = jnp.concatenate([jnp.where(sub < d, qt, zero), jnp.where(sub >= d, qt, zero)], axis=1)
    st = _dot(k2, qbd)
    base = nh if rev else 0
    m_new, ws = [], []
    for hd in range(2):
        r = base + 2 * hp + hd
        a_col = jnp.sum(jnp.where(lane == r, gcb, 0.0), axis=1, keepdims=True)
        a_row = gr_ref[j, pl.ds(r, 1), :]
        b_row = gr_ref[j, pl.ds(2 * nh + r, 1), :]
        tot = gr_ref[j, pl.ds(4 * nh + r, 1), :]
        amax = gr_ref[j, pl.ds(6 * nh + r, 1), :]
        m_h = m_state[hd]
        at = jnp.where(tri, a_col, NEG)
        mrow = jnp.maximum(jnp.max(at, axis=0, keepdims=True), m_h)
        p_ref[int(rev), tau, :, hd * L:(hd + 1) * L] = (
            st[:, hd * L:(hd + 1) * L] * jnp.exp2(at - mrow)).astype(BF16)
        m_end = jnp.maximum(m_h, amax)
        ws.append(jnp.exp2(a_row - m_end))
        rv_ref[int(rev), tau, 4 * hd:4 * hd + 4, :] = jnp.concatenate([
            jnp.exp2(m_h - mrow),
            jnp.exp2(-(b_row + mrow)),
            ws[hd],
            jnp.exp2(m_h - m_end),
        ], axis=0)
        m_new.append(tot + m_end)
    vf = vt_ref[j].astype(F32)
    vsc = jnp.concatenate(
        [(vf[hd * d:(hd + 1) * d] * ws[hd]).astype(BF16) for hd in range(2)]
        + [jnp.broadcast_to(ws[hd], (N_REP, L)).astype(BF16) for hd in range(2)], axis=0)
    u_ref[int(rev), tau] = jnp.where(_own_features(), _dot(vsc, k2), 0.0)
    return m_new


def _head_a_rows():
    row = lax.broadcasted_iota(jnp.int32, (2 * AUG, LANES), 0)
    return (row < HEAD_DIM) | ((row >= 2 * HEAD_DIM) & (row < 2 * HEAD_DIM + N_REP))


def _own_features():
    lane = lax.broadcasted_iota(jnp.int32, (2 * AUG, LANES), 1)
    return _head_a_rows() == (lane < HEAD_DIM)


def _mlstm_chunk(qt_ref, vt_ref, j, tau, ct_ref, p_ref, rv_ref, u_ref, hacc_ref, *, rev):
    L = MLSTM_L
    d = HEAD_DIM
    qt = qt_ref[j]
    vt = vt_ref[j]
    rv = rv_ref[int(rev), tau]
    wprev, eneg, _, wc = ([rv[4 * hd + i:4 * hd + i + 1] for hd in range(2)] for i in range(4))
    qf = qt.astype(F32)
    qw = jnp.concatenate([(qf * wprev[0]).astype(BF16), (qf * wprev[1]).astype(BF16)], axis=1)
    rhs = jnp.concatenate([p_ref[int(rev), tau], qw], axis=0)
    ones = jnp.ones((2 * N_REP, L), BF16)
    lhs = jnp.concatenate([jnp.concatenate([vt, ones], axis=0), ct_ref[...].astype(BF16)], axis=1)
    res = _dot(lhs, rhs)
    hs = []
    for hd in range(2):
        num = res[hd * d:(hd + 1) * d, hd * L:(hd + 1) * L]
        den = res[2 * d + hd * N_REP:2 * d + hd * N_REP + 1, hd * L:(hd + 1) * L]
        hs.append(num * (1.0 / jnp.maximum(jnp.abs(den), eneg[hd])))
    hacc_ref[int(rev), j] = jnp.concatenate(hs, axis=0)
    ct_ref[...] = jnp.where(_head_a_rows(), wc[0], wc[1]) * ct_ref[...] + u_ref[int(rev), tau]


def _mlstm_finish(hacc_ref, o_ref, g_ref, y_ref, n_chunks):
    sub = lax.broadcasted_iota(jnp.int32, (LANES, MLSTM_L), 0)
    row_a = sub < HEAD_DIM

    def head_mean(v):
        ma = jnp.mean(v[:HEAD_DIM], axis=0, keepdims=True)
        mb = jnp.mean(v[HEAD_DIM:], axis=0, keepdims=True)
        return jnp.where(row_a, ma, mb)

    def body(j, carry):
        rows = _chunk_rows(j)
        hs = hacc_ref[0, j] + hacc_ref[1, j]
        dv = hs - head_mean(hs)
        yn = (dv * lax.rsqrt(head_mean(dv * dv) + EPS)).T
        gate = _sigmoid(o_ref[rows, :].astype(F32))
        y_ref[rows, :] = (yn * g_ref[...] * gate).astype(y_ref.dtype)
        return carry

    lax.fori_loop(0, n_chunks, body, 0, unroll=min(4, n_chunks))


def _mlstm_kernel(qtx_ref, vtx_ref, kx_ref, ox_ref, gcx_ref, grx_ref,
                  qtc_ref, vtc_ref, kc_ref, oc_ref, gcc_ref, grc_ref, g_ref, *rest, need_ctx, nh):
    if need_ctx:
        y_ref, yc_ref, cf_ref, cb_ref, p_ref, rv_ref, u_ref, hx_ref, hc_ref = rest
    else:
        y_ref, cf_ref, cb_ref, p_ref, rv_ref, u_ref, hx_ref, hc_ref = rest
        yc_ref = None
    hp = pl.program_id(1)
    ncx = qtx_ref.shape[0]
    ncc = qtc_ref.shape[0]
    intra = functools.partial(_mlstm_intra, hp=hp, p_ref=p_ref, rv_ref=rv_ref, u_ref=u_ref, nh=nh)
    zero = jnp.zeros((1, MLSTM_L), F32)
    mf = [zero, zero]
    mb = [zero, zero]
    for j in range(ncc):
        mf = intra(qtc_ref, vtc_ref, kc_ref, gcc_ref, grc_ref, j, j, m_state=mf, rev=False)
        mb = intra(qtc_ref, vtc_ref, kc_ref, gcc_ref, grc_ref, ncc - 1 - j, j, m_state=mb, rev=True)

    def intra_body(j, carry):
        mf_ = intra(qtx_ref, vtx_ref, kx_ref, gcx_ref, grx_ref, j, ncc + j,
                    m_state=list(carry[0:2]), rev=False)
        mb_ = intra(qtx_ref, vtx_ref, kx_ref, gcx_ref, grx_ref, ncx - 1 - j, ncc + j,
                    m_state=list(carry[2:4]), rev=True)
        return (mf_[0], mf_[1], mb_[0], mb_[1])

    lax.fori_loop(0, ncx, intra_body, (mf[0], mf[1], mb[0], mb[1]), unroll=2)

    cf_ref[...] = jnp.zeros_like(cf_ref)
    cb_ref[...] = jnp.zeros_like(cb_ref)
    step = functools.partial(_mlstm_chunk, p_ref=p_ref, rv_ref=rv_ref, u_ref=u_ref)
    for j in range(ncc):
        step(qtc_ref, vtc_ref, j, j, cf_ref, hacc_ref=hc_ref, rev=False)
        step(qtc_ref, vtc_ref, ncc - 1 - j, j, cb_ref, hacc_ref=hc_ref, rev=True)

    def body(j, carry):
        step(qtx_ref, vtx_ref, j, ncc + j, cf_ref, hacc_ref=hx_ref, rev=False)
        step(qtx_ref, vtx_ref, ncx - 1 - j, ncc + j, cb_ref, hacc_ref=hx_ref, rev=True)
        return carry

    lax.fori_loop(0, ncx, body, 0, unroll=4)
    _mlstm_finish(hx_ref, ox_ref, g_ref, y_ref, ncx)
    if need_ctx:
        _mlstm_finish(hc_ref, oc_ref, g_ref, yc_ref, ncc)


def _mlstm(ptx, pfx, gcx, grx, ptc, pfc, gcc, grc, norm_g, *, need_ctx, d_grp):
    b, t, _ = ptx.shape
    ct = ptc.shape[1]
    nh = d_grp // HEAD_DIM
    n_pairs = d_grp // LANES
    n_steps = (t + ct) // MLSTM_L

    def seq_specs(tlen, pf, gr):
        nc = tlen // MLSTM_L
        return [
            pl.BlockSpec((None, nc, LANES, MLSTM_L), lambda bi, hp: (bi, 0, hp, 0)),
            pl.BlockSpec((None, nc, LANES, MLSTM_L), lambda bi, hp: (bi, 0, n_pairs + hp, 0)),
            pl.BlockSpec((None, tlen, LANES), lambda bi, hp: (bi, 0, hp)),
            pl.BlockSpec((None, tlen, LANES), lambda bi, hp: (bi, 0, n_pairs + hp)),
            pl.BlockSpec((None, tlen, LANES), lambda bi, hp: (bi, 0, 0)),
            pl.BlockSpec((None,) + gr.shape[1:], lambda bi, hp: (bi, 0, 0, 0)),
        ]

    out_specs = [pl.BlockSpec((None, t, LANES), lambda bi, hp: (bi, 0, hp))]
    out_shape = [jax.ShapeDtypeStruct((b, t, d_grp), BF16)]
    if need_ctx:
        out_specs.append(pl.BlockSpec((None, ct, LANES), lambda bi, hp: (bi, 0, hp)))
        out_shape.append(jax.ShapeDtypeStruct((b, ct, d_grp), BF16))
    kern = functools.partial(_mlstm_kernel, need_ctx=need_ctx, nh=nh)
    res = pl.pallas_call(
        kern,
        grid=(b, n_pairs),
        in_specs=seq_specs(t, pfx, grx) + seq_specs(ct, pfc, grc) + [
            pl.BlockSpec((1, LANES), lambda bi, hp: (0, hp))],
        out_specs=out_specs,
        out_shape=out_shape,
        scratch_shapes=[
            pltpu.VMEM((2 * AUG, LANES), F32),
            pltpu.VMEM((2 * AUG, LANES), F32),
            pltpu.VMEM((2, n_steps, MLSTM_L, 2 * MLSTM_L), BF16),
            pltpu.VMEM((2, n_steps, 8, MLSTM_L), F32),
            pltpu.VMEM((2, n_steps, 2 * AUG, LANES), F32),
            pltpu.VMEM((2, t // MLSTM_L, LANES, MLSTM_L), F32),
            pltpu.VMEM((2, ct // MLSTM_L, LANES, MLSTM_L), F32),
        ],
        compiler_params=pltpu.CompilerParams(
            dimension_semantics=("arbitrary", "arbitrary"), vmem_limit_bytes=VMEM_LIMIT),
        name="mlstm_ctx" if need_ctx else "mlstm",
    )(pfx, pfx, ptx, ptx, gcx, grx, pfc, pfc, ptc, ptc, gcc, grc, norm_g)
    return (res[0], res[1]) if need_ctx else (res[0], None)


def _softmax_pv(scores, values):
    m = None
    for s in scores:
        ms = jnp.max(s, axis=1, keepdims=True)
        m = ms if m is None else jnp.maximum(m, ms)
    acc, den = None, None
    for s, v in zip(scores, values):
        p = jnp.exp2(s - m)
        ds_ = jnp.sum(p, axis=1, keepdims=True)
        o = _dot(p.astype(BF16), v)
        acc = o if acc is None else acc + o
        den = ds_ if den is None else den + ds_
    return acc / den


def _na_kernel(q_ref, k_ref, v_ref, qc_ref, kc_ref, vc_ref, bt_ref, *rest, need_ctx, rows):
    if need_ctx:
        y_ref, yc_ref = rest
    else:
        (y_ref,) = rest
        yc_ref = None
    w = GRID_W
    qn = NA_QROWS * w
    kn = NA_WIN * w
    nblk = rows // NA_QROWS
    kc = kc_ref[...]
    vc = vc_ref[...]
    lane = lax.broadcasted_iota(jnp.int32, (qn, LANES), 1)
    head_a = lane < HEAD_DIM

    def body(ib, carry):
        r0 = ib * NA_QROWS
        u0 = jnp.clip(r0 - NA_ROWS // 2, 0, rows - NA_WIN)
        var = jnp.where(ib == 0, 0, jnp.where(ib == nblk - 1, 2, 1))
        qb = q_ref[pl.ds(pl.multiple_of(r0 * w, w), qn), :]
        kw = k_ref[pl.ds(pl.multiple_of(u0 * w, w), kn), :]
        vw = v_ref[pl.ds(pl.multiple_of(u0 * w, w), kn), :]
        outs = []
        for hd in range(2):
            hmask = head_a if hd == 0 else jnp.logical_not(head_a)
            qh = jnp.where(hmask, qb, jnp.zeros_like(qb))
            s_loc = _dot_nt(qh, kw) + bt_ref[var, hd]
            s_ctx = _dot_nt(qh, kc)
            outs.append(_softmax_pv([s_loc, s_ctx], [vw, vc]))
        y_ref[pl.ds(pl.multiple_of(r0 * w, w), qn), :] = (
            jnp.where(head_a, outs[0], outs[1]).astype(y_ref.dtype))
        return carry

    lax.fori_loop(0, nblk, body, 0)

    if need_ctx:
        qcb = qc_ref[...]
        lane_c = lax.broadcasted_iota(jnp.int32, qcb.shape, 1)
        head_ac = lane_c < HEAD_DIM
        outs = []
        for hd in range(2):
            hmask = head_ac if hd == 0 else jnp.logical_not(head_ac)
            qh = jnp.where(hmask, qcb, jnp.zeros_like(qcb))
            outs.append(_softmax_pv([_dot_nt(qh, kc)], [vc]))
        yc_ref[...] = jnp.where(head_ac, outs[0], outs[1]).astype(yc_ref.dtype)


def _na_bias_kernel(r_ref, o_ref):
    w = GRID_W
    lane = lax.broadcasted_iota(jnp.int32, (w, LANES), 1)
    q = lax.broadcasted_iota(jnp.int32, (w, LANES), 0)
    kcol = lane % w
    col_start = jnp.clip(q - NA_COLS // 2, 0, w - NA_COLS)
    col_ok = (kcol >= col_start) & (kcol < col_start + NA_COLS)
    left = lane < w
    lane_row = lax.broadcasted_iota(jnp.int32, (1, LANES), 1)
    neg = jnp.full((w, LANES), NEG, F32)
    cache = {}

    def pair_tile(dr_a, ok_a, ok_b):
        key = (dr_a, ok_a, ok_b)
        if key not in cache:
            if not (ok_a or ok_b):
                cache[key] = neg
            else:
                ra = r_ref[max(dr_a, 0):max(dr_a, 0) + 1, :]
                rb = r_ref[min(dr_a + 1, 2 * NA_ROWS - 2):min(dr_a + 1, 2 * NA_ROWS - 2) + 1, :]
                src = jnp.where(lane_row < w, ra, pltpu.roll(rb, w, axis=1))
                toep = pltpu.roll(jnp.broadcast_to(src, (w, LANES)), LANES - (NA_COLS - 1), axis=1,
                                  stride=1, stride_axis=0)
                ok = col_ok
                if not ok_a:
                    ok = ok & jnp.logical_not(left)
                if not ok_b:
                    ok = ok & left
                cache[key] = jnp.where(ok, toep * LOG2E, neg)
        return cache[key]

    offs = (0, NA_ROWS // 2, NA_ROWS)
    for v in range(3):
        for i in range(NA_QROWS):
            lo = (0, i, NA_QROWS)[v]
            for jj in range(NA_WIN // 2):
                j = 2 * jj
                dr_a = j - i + (NA_ROWS - 1) - offs[v]
                ok_a = lo <= j < lo + NA_ROWS
                ok_b = lo <= j + 1 < lo + NA_ROWS
                o_ref[v, i * w:(i + 1) * w, jj * LANES:(jj + 1) * LANES] = pair_tile(dr_a, ok_a, ok_b)


def _na_bias_table(rpb):
    depth, h, nr, nc = rpb.shape
    assert nr == 2 * NA_ROWS - 1 and nc == 2 * NA_COLS - 1 and 2 * GRID_W == LANES
    r_pad = jnp.pad(rpb.astype(F32), ((0, 0), (0, 0), (0, 16 - nr), (0, LANES - nc)))
    qn, kn = NA_QROWS * GRID_W, NA_WIN * GRID_W
    return pl.pallas_call(
        _na_bias_kernel,
        grid=(depth, h),
        in_specs=[pl.BlockSpec((None, None, 16, LANES), lambda l, hh: (l, hh, 0, 0))],
        out_specs=pl.BlockSpec((None, 3, None, qn, kn), lambda l, hh: (l, 0, hh, 0, 0)),
        out_shape=jax.ShapeDtypeStruct((depth, 3, h, qn, kn), F32),
        compiler_params=pltpu.CompilerParams(
            dimension_semantics=("arbitrary", "arbitrary"), vmem_limit_bytes=VMEM_LIMIT),
        name="natten_bias",
    )(r_pad)


def _na(px, pc, btab, *, need_ctx, d_grp, col0):
    b, t, _ = px.shape
    ct = pc.shape[1]
    n_pairs = d_grp // LANES
    per = d_grp // LANES
    rows = t // GRID_W

    def col(k):
        return lambda bi, hp: (bi, 0, col0 + k * per + hp)

    in_specs = ([pl.BlockSpec((None, t, LANES), col(k)) for k in range(3)]
                + [pl.BlockSpec((None, ct, LANES), col(k)) for k in range(3)]
                + [pl.BlockSpec((3, 2) + btab.shape[2:], lambda bi, hp: (0, hp, 0, 0))])
    out_specs = [pl.BlockSpec((None, t, LANES), lambda bi, hp: (bi, 0, hp))]
    out_shape = [jax.ShapeDtypeStruct((b, t, d_grp), BF16)]
    if need_ctx:
        out_specs.append(pl.BlockSpec((None, ct, LANES), lambda bi, hp: (bi, 0, hp)))
        out_shape.append(jax.ShapeDtypeStruct((b, ct, d_grp), BF16))
    kern = functools.partial(_na_kernel, need_ctx=need_ctx, rows=rows)
    res = pl.pallas_call(
        kern,
        grid=(b, n_pairs),
        in_specs=in_specs,
        out_specs=out_specs,
        out_shape=out_shape,
        compiler_params=pltpu.CompilerParams(
            dimension_semantics=("arbitrary", "arbitrary"), vmem_limit_bytes=VMEM_LIMIT),
        name="natten_ctx" if need_ctx else "natten",
    )(px, px, px, pc, pc, pc, btab)
    return (res[0], res[1]) if need_ctx else (res[0], None)


def _out_mlp_kernel(x_ref, ym_ref, yn_ref, mod_ref, g_ref, wo_ref, w1_ref, w2_ref, fg_ref, o_ref,
                    *, d_model, final_norm):
    d = d_model
    dm = ym_ref.shape[1]
    att = _dot(ym_ref[...], wo_ref[0:dm, :]) + _dot(yn_ref[...], wo_ref[dm:, :])
    x1 = x_ref[...] + mod_ref[:, 2 * d:3 * d] * att
    h = _modulated_norm(x1, g_ref[...], mod_ref[:, 4 * d:5 * d], mod_ref[:, 3 * d:4 * d]).astype(BF16)
    d_ff = w1_ref.shape[1]
    acc = None
    for c in range(d_ff // COL_GROUP):
        hid = _dot(h, w1_ref[:, c * COL_GROUP:(c + 1) * COL_GROUP])
        hid = jnp.square(jnp.maximum(hid, 0.0)).astype(BF16)
        part = _dot(hid, w2_ref[c * COL_GROUP:(c + 1) * COL_GROUP, :])
        acc = part if acc is None else acc + part
    x2 = x1 + mod_ref[:, 5 * d:6 * d] * acc
    if final_norm:
        x2 = (x2 * lax.rsqrt(jnp.mean(x2 * x2, axis=-1, keepdims=True) + EPS)) * fg_ref[...]
    o_ref[...] = x2


def _out_mlp(x, ym, yn, mod4, mod_row, g, w_out, w1, w2, final_g, *, final_norm):
    b, t, d = x.shape
    dm = ym.shape[2]
    d_ff = w1.shape[1]
    tm = min(TOKEN_TILE, t)
    kern = functools.partial(_out_mlp_kernel, d_model=d, final_norm=final_norm)
    const = lambda i, bi: (0, 0)
    return pl.pallas_call(
        kern,
        grid=(t // tm, b),
        in_specs=[
            pl.BlockSpec((None, tm, d), lambda i, bi: (bi, i, 0)),
            pl.BlockSpec((None, tm, dm), lambda i, bi: (bi, i, 0)),
            pl.BlockSpec((None, tm, dm), lambda i, bi: (bi, i, 0)),
            pl.BlockSpec((None, 1, N_MOD * d), lambda i, bi: (mod_row(bi), 0, 0)),
            pl.BlockSpec((1, d), const),
            pl.BlockSpec((d, d), const, pipeline_mode=pl.Buffered(1)),
            pl.BlockSpec((d, d_ff), const, pipeline_mode=pl.Buffered(1)),
            pl.BlockSpec((d_ff, d), const, pipeline_mode=pl.Buffered(1)),
            pl.BlockSpec((1, d), const),
        ],
        out_specs=pl.BlockSpec((None, tm, d), lambda i, bi: (bi, i, 0)),
        out_shape=jax.ShapeDtypeStruct((b, t, d), F32),
        compiler_params=pltpu.CompilerParams(
            dimension_semantics=("arbitrary", "arbitrary"), vmem_limit_bytes=VMEM_LIMIT),
        name="out_mlp_final" if final_norm else "out_mlp",
    )(x, ym, yn, mod4, g, w_out, w1, w2, final_g)


def _rope_tables(n_tokens):
    t = jnp.arange(n_tokens)
    row = (t // GRID_W).astype(F32)
    col = (t % GRID_W).astype(F32)
    n_freq = HEAD_DIM // 4
    inv_freq = ROPE_BASE ** (-jnp.arange(n_freq, dtype=F32) / n_freq)
    ang = jnp.concatenate([row[:, None] * inv_freq, col[:, None] * inv_freq], axis=-1)
    cos, sin = jnp.cos(ang), jnp.sin(ang)
    reps = LANES // HEAD_DIM
    return (jnp.tile(jnp.concatenate([cos, cos], axis=-1), (1, reps)),
            jnp.tile(jnp.concatenate([-sin, sin], axis=-1), (1, reps)))


def kernel(x, c, ctx, c_ctx, w_ada, b_ada, norm1_g, w_in, b_gate, mlstm_norm_g, rpb, w_out, norm2_g,
           w_mlp1, w_mlp2, final_g):
    b, s, d = x.shape
    ct = ctx.shape[1]
    depth = w_ada.shape[0]
    d_grp = d // 2
    nh = d_grp // HEAD_DIM
    n_gate = b_gate.shape[1]
    n_main = w_in.shape[2] - n_gate
    assert n_main == 7 * d_grp and d_grp % LANES == 0 and n_gate == 4 * nh and n_gate <= LANES
    assert s % TOKEN_TILE == 0 and ct % MLSTM_L == 0 and (s // GRID_W) % NA_QROWS == 0

    n_rows = -(-(b + 1) // 8) * 8
    cc = jnp.zeros((n_rows, d), F32).at[:b].set(c).at[b].set(c_ctx)
    mod = _adaln(cc, w_ada, b_ada)
    mod = mod.reshape(depth, n_rows, 1, N_MOD * d)

    cos_t, sin_t = _rope_tables(s)
    grp = lambda k: w_in[:, :, k * d_grp:(k + 1) * d_grp]
    w_tok = jnp.concatenate([grp(1), grp(3), grp(4), grp(5), grp(6)], axis=2).astype(BF16)
    w_feat = jnp.swapaxes(jnp.concatenate([grp(0), grp(2)], axis=2), 1, 2).astype(BF16)
    w_gate = jnp.pad(w_in[:, :, n_main:], ((0, 0), (0, 0), (0, LANES - n_gate))).astype(BF16)
    b_gate_p = jnp.pad(b_gate.astype(F32), ((0, 0), (0, LANES - n_gate))).reshape(depth, 1, LANES)
    w_out_b = w_out.astype(BF16)
    w1_b = w_mlp1.astype(BF16)
    w2_b = w_mlp2.astype(BF16)
    na_col0 = 2 * (d_grp // LANES)
    btab = _na_bias_table(rpb)

    x_row = lambda bi: bi
    c_row = lambda bi: b
    xc = ctx
    for l in range(depth):
        need_ctx = l < depth - 1
        g1 = norm1_g[l].reshape(1, d)
        g2 = norm2_g[l].reshape(1, d)
        ptx, pfx, gcx, grx = _in_proj(x, mod[l], x_row, g1, w_tok[l], w_feat[l], w_gate[l], b_gate_p[l],
                                      cos_t, sin_t, rope=True, nh=nh)
        ptc, pfc, gcc, grc = _in_proj(xc, mod[l], c_row, g1, w_tok[l], w_feat[l], w_gate[l], b_gate_p[l],
                                      cos_t[:ct], sin_t[:ct], rope=False, nh=nh)
        ym, ycm = _mlstm(ptx, pfx, gcx, grx, ptc, pfc, gcc, grc, mlstm_norm_g[l].reshape(1, d_grp),
                         need_ctx=need_ctx, d_grp=d_grp)
        yn, ycn = _na(ptx, ptc, btab[l], need_ctx=need_ctx, d_grp=d_grp, col0=na_col0)
        fg = final_g.reshape(1, d)
        x = _out_mlp(x, ym, yn, mod[l], x_row, g2, w_out_b[l], w1_b[l], w2_b[l], fg,
                     final_norm=(l == depth - 1))
        if need_ctx:
            xc = _out_mlp(xc, ycm, ycn, mod[l], c_row, g2, w_out_b[l], w1_b[l], w2_b[l], fg,
                          final_norm=False)
    return x
```

```python
import functools

import jax
import jax.numpy as jnp
import numpy as np
from jax import lax
from jax.experimental import pallas as pl
from jax.experimental.pallas import tpu as pltpu

HEAD_DIM = 64
GRID_W = 64
NA_ROWS = 8
NA_COLS = 16
N_MOD = 6
ROPE_BASE = 10000.0
EPS = 1e-6

LANES = 128
MLSTM_L = 128
NA_QROWS = 4
NA_WIN = NA_QROWS + NA_ROWS
NEG = -1e30
LOG2E = 1.4426950408889634
COL_GROUP = 512
TOKEN_TILE = 512
VMEM_LIMIT = 56 * 1024 * 1024

F32 = jnp.float32
BF16 = jnp.bfloat16


def _dot(a, b):
    return jnp.dot(a, b, preferred_element_type=F32)


def _dot_nt(a, b):
    return lax.dot_general(a, b, (((1,), (1,)), ((), ())), preferred_element_type=F32)


def _split3(a):
    p0 = a.astype(BF16)
    r1 = a - p0.astype(F32)
    p1 = r1.astype(BF16)
    p2 = (r1 - p1.astype(F32)).astype(BF16)
    return p0, p1, p2


def _sigmoid(v):
    return 1.0 / (1.0 + jnp.exp(-v))


def _log_sigmoid(v):
    return jnp.minimum(v, 0.0) - jnp.log(1.0 + jnp.exp(-jnp.abs(v)))


def _adaln_kernel(c_ref, w_ref, b_ref, o_ref):
    c = c_ref[...]
    s = (c * _sigmoid(c)).astype(BF16)
    o_ref[...] = _dot(s, w_ref[...].astype(BF16)) + b_ref[...]


def _adaln(cc, w_ada, b_ada):
    depth, d, n = w_ada.shape
    rows = cc.shape[0]
    tn = n // 4
    return pl.pallas_call(
        _adaln_kernel,
        grid=(depth, n // tn),
        in_specs=[
            pl.BlockSpec((rows, d), lambda l, j: (0, 0)),
            pl.BlockSpec((None, d, tn), lambda l, j: (l, 0, j)),
            pl.BlockSpec((None, 1, tn), lambda l, j: (l, 0, j)),
        ],
        out_specs=pl.BlockSpec((None, rows, tn), lambda l, j: (l, 0, j)),
        out_shape=jax.ShapeDtypeStruct((depth, rows, n), F32),
        compiler_params=pltpu.CompilerParams(
            dimension_semantics=("arbitrary", "arbitrary"), vmem_limit_bytes=VMEM_LIMIT),
        name="adaln",
    )(cc, w_ada, b_ada.reshape(depth, 1, n))


def _modulated_norm(x, g, scale, shift):
    y = x * lax.rsqrt(jnp.mean(x * x, axis=-1, keepdims=True) + EPS)
    return (y * g) * (1.0 + scale) + shift


def _in_proj_kernel(x_ref, mod_ref, g_ref, wt_ref, wf_ref, wg_ref, bg_ref, cos_ref, sin_ref,
                    pt_ref, pf_ref, gc_ref, gr_ref, *, d_model, rope, nh):
    d = d_model
    tm = x_ref.shape[0]
    L = MLSTM_L
    n_chunks = tm // L
    h = _modulated_norm(x_ref[...], g_ref[...], mod_ref[:, d:2 * d], mod_ref[:, 0:d])
    hb = h.astype(BF16)
    q_scale = HEAD_DIM ** -0.5
    half = HEAD_DIM // 2

    gt = _dot(hb, wg_ref[...]) + bg_ref[...]
    lf = _log_sigmoid(gt)
    t_ = lax.broadcasted_iota(jnp.int32, (L, L), 0)
    u_ = lax.broadcasted_iota(jnp.int32, (L, L), 1)
    tri_prefix = (u_ <= t_).astype(BF16)
    lane_c = lax.broadcasted_iota(jnp.int32, (L, LANES), 1)
    for c in range(n_chunks):
        lfc = lf[c * L:(c + 1) * L]
        b_pre = sum(_dot(tri_prefix, p) for p in _split3(lfc))
        tot = jnp.broadcast_to(jnp.sum(lfc, axis=0, keepdims=True), (L, LANES))
        b_suf = tot - b_pre + lfc
        b_sel = jnp.where(lane_c < 3 * nh, b_pre, b_suf)
        a = gt[c * L:(c + 1) * L] - pltpu.roll(b_sel, LANES - 2 * nh, axis=1)
        amax = jnp.broadcast_to(jnp.max(a, axis=0, keepdims=True), (L, LANES))
        out = jnp.where(lane_c < 2 * nh, a,
              jnp.where(lane_c < 4 * nh, b_sel,
              jnp.where(lane_c < 6 * nh, pltpu.roll(tot, 2 * nh, axis=1),
                        pltpu.roll(amax, 6 * nh, axis=1))))
        out = out * LOG2E
        gc_ref[c * L:(c + 1) * L, :] = out
        gr_ref[c] = out.T[0:8 * nh]

    lane = lax.broadcasted_iota(jnp.int32, (tm, LANES), 1)
    first_half = (lane % HEAD_DIM) < half
    for cg in range(wt_ref.shape[1] // COL_GROUP):
        acc = _dot(hb, wt_ref[:, cg * COL_GROUP:(cg + 1) * COL_GROUP])
        for s in range(COL_GROUP // LANES):
            slab = acc[:, s * LANES:(s + 1) * LANES]
            if rope and cg == 0:
                partner = jnp.where(first_half,
                                    pltpu.roll(slab, LANES - half, axis=1),
                                    pltpu.roll(slab, half, axis=1))
                slab = slab * cos_ref[...] + partner * sin_ref[...]
            c0 = cg * COL_GROUP + s * LANES
            pt_ref[:, c0:c0 + LANES] = slab.astype(BF16)

    d_grp = COL_GROUP
    part_scale = {0: q_scale, 2: q_scale * LOG2E}
    if rope:
        cos_t = cos_ref[...].T[0:half]
        sin_t = sin_ref[...].T[half:HEAD_DIM]
    for part in range(wf_ref.shape[0] // d_grp):
        acc = _dot_nt(wf_ref[part * d_grp:(part + 1) * d_grp, :], hb)
        for hh in range(d_grp // HEAD_DIM):
            r0 = hh * HEAD_DIM
            if part == 0 and rope:
                x1 = acc[r0:r0 + half]
                x2 = acc[r0 + half:r0 + HEAD_DIM]
                blk = jnp.concatenate([x1 * cos_t - x2 * sin_t, x1 * sin_t + x2 * cos_t], axis=0)
            else:
                blk = acc[r0:r0 + HEAD_DIM]
            if part in part_scale:
                blk = blk * part_scale[part]
            blk = blk.astype(BF16)
            for c in range(n_chunks):
                pf_ref[c, part * d_grp + r0:part * d_grp + r0 + HEAD_DIM, :] = blk[:, c * L:(c + 1) * L]


def _in_proj(x, mod4, mod_row, g, w_tok, w_feat, w_gate, b_gate, cos_t, sin_t, *, rope, nh):
    b, t, d = x.shape
    nt = w_tok.shape[1]
    nf = w_feat.shape[0]
    tm = min(TOKEN_TILE, t)
    nc = tm // MLSTM_L
    kern = functools.partial(_in_proj_kernel, d_model=d, rope=rope, nh=nh)
    const = lambda i, bi: (0, 0)
    return pl.pallas_call(
        kern,
        grid=(t // tm, b),
        in_specs=[
            pl.BlockSpec((None, tm, d), lambda i, bi: (bi, i, 0)),
            pl.BlockSpec((None, 1, N_MOD * d), lambda i, bi: (mod_row(bi), 0, 0)),
            pl.BlockSpec((1, d), const),
            pl.BlockSpec((d, nt), const, pipeline_mode=pl.Buffered(1)),
            pl.BlockSpec((nf, d), const, pipeline_mode=pl.Buffered(1)),
            pl.BlockSpec((d, LANES), const),
            pl.BlockSpec((1, LANES), const),
            pl.BlockSpec((tm, LANES), lambda i, bi: (i, 0)),
            pl.BlockSpec((tm, LANES), lambda i, bi: (i, 0)),
        ],
        out_specs=[
            pl.BlockSpec((None, tm, nt), lambda i, bi: (bi, i, 0)),
            pl.BlockSpec((None, nc, nf, MLSTM_L), lambda i, bi: (bi, i, 0, 0)),
            pl.BlockSpec((None, tm, LANES), lambda i, bi: (bi, i, 0)),
            pl.BlockSpec((None, nc, 8 * nh, MLSTM_L), lambda i, bi: (bi, i, 0, 0)),
        ],
        out_shape=[
            jax.ShapeDtypeStruct((b, t, nt), BF16),
            jax.ShapeDtypeStruct((b, t // MLSTM_L, nf, MLSTM_L), BF16),
            jax.ShapeDtypeStruct((b, t, LANES), F32),
            jax.ShapeDtypeStruct((b, t // MLSTM_L, 8 * nh, MLSTM_L), F32),
        ],
        compiler_params=pltpu.CompilerParams(
            dimension_semantics=("arbitrary", "arbitrary"), vmem_limit_bytes=VMEM_LIMIT),
        name="in_proj_rope" if rope else "in_proj",
    )(x, mod4, g, w_tok, w_feat, w_gate, b_gate, cos_t, sin_t)


def _chunk_rows(j):
    if isinstance(j, int):
        return pl.ds(j * MLSTM_L, MLSTM_L)
    return pl.ds(pl.multiple_of(j * MLSTM_L, MLSTM_L), MLSTM_L)


N_REP = 16
AUG = HEAD_DIM + N_REP


def _mlstm_intra(qt_ref, vt_ref, k_ref, gc_ref, gr_ref, j, tau, hp, m_state, p_ref, rv_ref, u_ref,
                 *, rev, nh):
    L = MLSTM_L
    d = HEAD_DIM
    rows = _chunk_rows(j)
    gcb = gc_ref[rows, :]
    qt = qt_ref[j]
    k2 = k_ref[rows, :]
    lane = lax.broadcasted_iota(jnp.int32, (L, LANES), 1)
    sub = lax.broadcasted_iota(jnp.int32, (L, LANES), 0)
    tri = (sub >= lane) if rev else (sub <= lane)
    zero = jnp.zeros_like(qt)
    qbd = jnp.concatenate([jnp.where(sub < d, qt, zero), jnp.where(sub >= d, qt, zero)], axis=1)
    st = _dot(k2, qbd)
    base = nh if rev else 0
    m_new, ws = [], []
    for hd in range(2):
        r = base + 2 * hp + hd
        a_col = jnp.sum(jnp.where(lane == r, gcb, 0.0), axis=1, keepdims=True)
        a_row = gr_ref[j, pl.ds(r, 1), :]
        b_row = gr_ref[j, pl.ds(2 * nh + r, 1), :]
        tot = gr_ref[j, pl.ds(4 * nh + r, 1), :]
        amax = gr_ref[j, pl.ds(6 * nh + r, 1), :]
        m_h = m_state[hd]
        at = jnp.where(tri, a_col, NEG)
        mrow = jnp.maximum(jnp.max(at, axis=0, keepdims=True), m_h)
        p_ref[int(rev), tau, :, hd * L:(hd + 1) * L] = (
            st[:, hd * L:(hd + 1) * L] * jnp.exp2(at - mrow)).astype(BF16)
        m_end = jnp.maximum(m_h, amax)
        ws.append(jnp.exp2(a_row - m_end))
        rv_ref[int(rev), tau, 4 * hd:4 * hd + 4, :] = jnp.concatenate([
            jnp.exp2(m_h - mrow),
            jnp.exp2(-(b_row + mrow)),
            ws[hd],
            jnp.exp2(m_h - m_end),
        ], axis=0)
        m_new.append(tot + m_end)
    vf = vt_ref[j].astype(F32)
    vsc = jnp.concatenate(
        [(vf[hd * d:(hd + 1) * d] * ws[hd]).astype(BF16) for hd in range(2)]
        + [jnp.broadcast_to(ws[hd], (N_REP, L)).astype(BF16) for hd in range(2)], axis=0)
    u_ref[int(rev), tau] = jnp.where(_own_features(), _dot(vsc, k2), 0.0)
    return m_new


def _head_a_rows():
    row = lax.broadcasted_iota(jnp.int32, (2 * AUG, LANES), 0)
    return (row < HEAD_DIM) | ((row >= 2 * HEAD_DIM) & (row < 2 * HEAD_DIM + N_REP))


def _own_features():
    lane = lax.broadcasted_iota(jnp.int32, (2 * AUG, LANES), 1)
    return _head_a_rows() == (lane < HEAD_DIM)


def _mlstm_chunk(qt_ref, vt_ref, j, tau, ct_ref, p_ref, rv_ref, u_ref, hacc_ref, *, rev):
    L = MLSTM_L
    d = HEAD_DIM
    qt = qt_ref[j]
    vt = vt_ref[j]
    rv = rv_ref[int(rev), tau]
    wprev, eneg, _, wc = ([rv[4 * hd + i:4 * hd + i + 1] for hd in range(2)] for i in range(4))
    qf = qt.astype(F32)
    qw = jnp.concatenate([(qf * wprev[0]).astype(BF16), (qf * wprev[1]).astype(BF16)], axis=1)
    rhs = jnp.concatenate([p_ref[int(rev), tau], qw], axis=0)
    ones = jnp.ones((2 * N_REP, L), BF16)
    lhs = jnp.concatenate([jnp.concatenate([vt, ones], axis=0), ct_ref[...].astype(BF16)], axis=1)
    res = _dot(lhs, rhs)
    hs = []
    for hd in range(2):
        num = res[hd * d:(hd + 1) * d, hd * L:(hd + 1) * L]
        den = res[2 * d + hd * N_REP:2 * d + hd * N_REP + 1, hd * L:(hd + 1) * L]
        hs.append(num * (1.0 / jnp.maximum(jnp.abs(den), eneg[hd])))
    hacc_ref[int(rev), j] = jnp.concatenate(hs, axis=0)
    ct_ref[...] = jnp.where(_head_a_rows(), wc[0], wc[1]) * ct_ref[...] + u_ref[int(rev), tau]


def _mlstm_finish(hacc_ref, o_ref, g_ref, y_ref, n_chunks):
    sub = lax.broadcasted_iota(jnp.int32, (LANES, MLSTM_L), 0)
    row_a = sub < HEAD_DIM

    def head_mean(v):
        ma = jnp.mean(v[:HEAD_DIM], axis=0, keepdims=True)
        mb = jnp.mean(v[HEAD_DIM:], axis=0, keepdims=True)
        return jnp.where(row_a, ma, mb)

    def body(j, carry):
        rows = _chunk_rows(j)
        hs = hacc_ref[0, j] + hacc_ref[1, j]
        dv = hs - head_mean(hs)
        yn = (dv * lax.rsqrt(head_mean(dv * dv) + EPS)).T
        gate = _sigmoid(o_ref[rows, :].astype(F32))
        y_ref[rows, :] = (yn * g_ref[...] * gate).astype(y_ref.dtype)
        return carry

    lax.fori_loop(0, n_chunks, body, 0, unroll=min(4, n_chunks))


def _mlstm_kernel(qtx_ref, vtx_ref, kx_ref, ox_ref, gcx_ref, grx_ref,
                  qtc_ref, vtc_ref, kc_ref, oc_ref, gcc_ref, grc_ref, g_ref, *rest, need_ctx, nh):
    if need_ctx:
        y_ref, yc_ref, cf_ref, cb_ref, p_ref, rv_ref, u_ref, hx_ref, hc_ref = rest
    else:
        y_ref, cf_ref, cb_ref, p_ref, rv_ref, u_ref, hx_ref, hc_ref = rest
        yc_ref = None
    hp = pl.program_id(1)
    ncx = qtx_ref.shape[0]
    ncc = qtc_ref.shape[0]
    intra = functools.partial(_mlstm_intra, hp=hp, p_ref=p_ref, rv_ref=rv_ref, u_ref=u_ref, nh=nh)
    zero = jnp.zeros((1, MLSTM_L), F32)
    mf = [zero, zero]
    mb = [zero, zero]
    for j in range(ncc):
        mf = intra(qtc_ref, vtc_ref, kc_ref, gcc_ref, grc_ref, j, j, m_state=mf, rev=False)
        mb = intra(qtc_ref, vtc_ref, kc_ref, gcc_ref, grc_ref, ncc - 1 - j, j, m_state=mb, rev=True)

    def intra_body(j, carry):
        mf_ = intra(qtx_ref, vtx_ref, kx_ref, gcx_ref, grx_ref, j, ncc + j,
                    m_state=list(carry[0:2]), rev=False)
        mb_ = intra(qtx_ref, vtx_ref, kx_ref, gcx_ref, grx_ref, ncx - 1 - j, ncc + j,
                    m_state=list(carry[2:4]), rev=True)
        return (mf_[0], mf_[1], mb_[0], mb_[1])

    lax.fori_loop(0, ncx, intra_body, (mf[0], mf[1], mb[0], mb[1]), unroll=2)

    cf_ref[...] = jnp.zeros_like(cf_ref)
    cb_ref[...] = jnp.zeros_like(cb_ref)
    step = functools.partial(_mlstm_chunk, p_ref=p_ref, rv_ref=rv_ref, u_ref=u_ref)
    for j in range(ncc):
        step(qtc_ref, vtc_ref, j, j, cf_ref, hacc_ref=hc_ref, rev=False)
        step(qtc_ref, vtc_ref, ncc - 1 - j, j, cb_ref, hacc_ref=hc_ref, rev=True)

    def body(j, carry):
        step(qtx_ref, vtx_ref, j, ncc + j, cf_ref, hacc_ref=hx_ref, rev=False)
        step(qtx_ref, vtx_ref, ncx - 1 - j, ncc + j, cb_ref, hacc_ref=hx_ref, rev=True)
        return carry

    lax.fori_loop(0, ncx, body, 0, unroll=4)
    _mlstm_finish(hx_ref, ox_ref, g_ref, y_ref, ncx)
    if need_ctx:
        _mlstm_finish(hc_ref, oc_ref, g_ref, yc_ref, ncc)


def _mlstm(ptx, pfx, gcx, grx, ptc, pfc, gcc, grc, norm_g, *, need_ctx, d_grp):
    b, t, _ = ptx.shape
    ct = ptc.shape[1]
    nh = d_grp // HEAD_DIM
    n_pairs = d_grp // LANES
    n_steps = (t + ct) // MLSTM_L

    def seq_specs(tlen, pf, gr):
        nc = tlen // MLSTM_L
        return [
            pl.BlockSpec((None, nc, LANES, MLSTM_L), lambda bi, hp: (bi, 0, hp, 0)),
            pl.BlockSpec((None, nc, LANES, MLSTM_L), lambda bi, hp: (bi, 0, n_pairs + hp, 0)),
            pl.BlockSpec((None, tlen, LANES), lambda bi, hp: (bi, 0, hp)),
            pl.BlockSpec((None, tlen, LANES), lambda bi, hp: (bi, 0, n_pairs + hp)),
            pl.BlockSpec((None, tlen, LANES), lambda bi, hp: (bi, 0, 0)),
            pl.BlockSpec((None,) + gr.shape[1:], lambda bi, hp: (bi, 0, 0, 0)),
        ]

    out_specs = [pl.BlockSpec((None, t, LANES), lambda bi, hp: (bi, 0, hp))]
    out_shape = [jax.ShapeDtypeStruct((b, t, d_grp), BF16)]
    if need_ctx:
        out_specs.append(pl.BlockSpec((None, ct, LANES), lambda bi, hp: (bi, 0, hp)))
        out_shape.append(jax.ShapeDtypeStruct((b, ct, d_grp), BF16))
    kern = functools.partial(_mlstm_kernel, need_ctx=need_ctx, nh=nh)
    res = pl.pallas_call(
        kern,
        grid=(b, n_pairs),
        in_specs=seq_specs(t, pfx, grx) + seq_specs(ct, pfc, grc) + [
            pl.BlockSpec((1, LANES), lambda bi, hp: (0, hp))],
        out_specs=out_specs,
        out_shape=out_shape,
        scratch_shapes=[
            pltpu.VMEM((2 * AUG, LANES), F32),
            pltpu.VMEM((2 * AUG, LANES), F32),
            pltpu.VMEM((2, n_steps, MLSTM_L, 2 * MLSTM_L), BF16),
            pltpu.VMEM((2, n_steps, 8, MLSTM_L), F32),
            pltpu.VMEM((2, n_steps, 2 * AUG, LANES), F32),
            pltpu.VMEM((2, t // MLSTM_L, LANES, MLSTM_L), F32),
            pltpu.VMEM((2, ct // MLSTM_L, LANES, MLSTM_L), F32),
        ],
        compiler_params=pltpu.CompilerParams(
            dimension_semantics=("arbitrary", "arbitrary"), vmem_limit_bytes=VMEM_LIMIT),
        name="mlstm_ctx" if need_ctx else "mlstm",
    )(pfx, pfx, ptx, ptx, gcx, grx, pfc, pfc, ptc, ptc, gcc, grc, norm_g)
    return (res[0], res[1]) if need_ctx else (res[0], None)


NA_KEY_GROUP = 256


def _na_softmax(s_ref, p_ref, idx, n_keys, n_q):
    sums = []
    for lt in range(n_q // LANES):
        cols = slice(lt * LANES, (lt + 1) * LANES)
        m = None
        for kg in range(n_keys // NA_KEY_GROUP):
            mk = jnp.max(s_ref[idx + (slice(kg * NA_KEY_GROUP, (kg + 1) * NA_KEY_GROUP), cols)],
                         axis=0, keepdims=True)
            m = mk if m is None else jnp.maximum(m, mk)
        tot = None
        for kg in range(n_keys // NA_KEY_GROUP):
            r = slice(kg * NA_KEY_GROUP, (kg + 1) * NA_KEY_GROUP)
            p = jnp.exp2(s_ref[idx + (r, cols)] - m)
            p_ref[idx + (r, cols)] = p.astype(BF16)
            sk = jnp.sum(p, axis=0, keepdims=True)
            tot = sk if tot is None else tot + sk
        sums.append(tot)
    return jnp.concatenate(sums, axis=1)


def _na_kernel(qt_ref, vt_ref, k_ref, qtc_ref, vtc_ref, kc_ref, bt_ref, *rest, need_ctx, rows):
    if need_ctx:
        y_ref, yc_ref, s_ref, p_ref, l_ref = rest
    else:
        y_ref, s_ref, p_ref, l_ref = rest
        yc_ref = None
    w = GRID_W
    L = MLSTM_L
    qn = NA_QROWS * w
    kn = NA_WIN * w
    n_ctx = kc_ref.shape[0]
    nblk = rows // NA_QROWS
    kc = kc_ref[...]
    vtc = jnp.concatenate([vtc_ref[c] for c in range(n_ctx // L)], axis=1)
    sub = lax.broadcasted_iota(jnp.int32, (LANES, qn), 0)
    row_a = sub < HEAD_DIM

    def first_key_row(ib):
        if isinstance(ib, int):
            return min(max(ib * NA_QROWS - NA_ROWS // 2, 0), rows - NA_WIN)
        return jnp.clip(ib * NA_QROWS - NA_ROWS // 2, 0, rows - NA_WIN)

    def token_rows(start, n):
        if isinstance(start, int):
            return pl.ds(start, n)
        return pl.ds(pl.multiple_of(start, w), n)

    def chunks(ref, tok0, n_tok):
        c0 = tok0 // L
        return jnp.concatenate([ref[c0 + c] for c in range(n_tok // L)], axis=1)

    def head_queries(qt, hd):
        return jnp.where(row_a if hd == 0 else jnp.logical_not(row_a), qt, jnp.zeros_like(qt))

    def scores(ib, buf):
        if isinstance(ib, int):
            var = 0 if ib == 0 else (2 if ib == nblk - 1 else 1)
        else:
            var = jnp.where(ib == nblk - 1, 2, 1)
        qt = chunks(qt_ref, ib * qn, qn)
        kw = k_ref[token_rows(first_key_row(ib) * w, kn), :]
        for hd in range(2):
            qh = head_queries(qt, hd)
            s_ref[buf, hd, 0:kn, :] = _dot(kw, qh) + bt_ref[var, hd]
            s_ref[buf, hd, kn:, :] = _dot(kc, qh)

    def softmax(buf):
        for hd in range(2):
            l_ref[buf, hd] = jnp.broadcast_to(
                _na_softmax(s_ref, p_ref, (buf, hd), kn + n_ctx, qn), (8, qn))

    def values(ib, buf):
        vt = jnp.concatenate([chunks(vt_ref, first_key_row(ib) * w, kn), vtc], axis=1)
        outs = [_dot(vt, p_ref[buf, hd]) * (1.0 / l_ref[buf, hd, 0:1, :]) for hd in range(2)]
        y_ref[token_rows(ib * qn, qn), :] = jnp.where(row_a, outs[0], outs[1]).T.astype(y_ref.dtype)

    scores(0, 0)
    scores(1, 1)
    softmax(0)

    def body(tt, carry):
        t = 2 * tt + 1
        scores(t + 1, 0)
        softmax(1)
        values(t - 1, 0)
        scores(t + 2, 1)
        softmax(0)
        values(t, 1)
        return carry

    lax.fori_loop(0, (nblk - 2) // 2, body, 0)
    softmax(1)
    values(nblk - 2, 0)
    values(nblk - 1, 1)

    if need_ctx:
        qtc = jnp.concatenate([qtc_ref[c] for c in range(n_ctx // L)], axis=1)
        sub_c = lax.broadcasted_iota(jnp.int32, qtc.shape, 0)
        row_ac = sub_c < HEAD_DIM
        for c0 in range(0, n_ctx, qn):
            outs = []
            for hd in range(2):
                qh = jnp.where(row_ac if hd == 0 else jnp.logical_not(row_ac), qtc, jnp.zeros_like(qtc))
                s_ref[0, hd, 0:n_ctx, :] = _dot(kc, qh[:, c0:c0 + qn])
                lsum = _na_softmax(s_ref, p_ref, (0, hd), n_ctx, qn)
                outs.append(_dot(vtc, p_ref[0, hd, 0:n_ctx, :]) * (1.0 / lsum))
            yc_ref[c0:c0 + qn, :] = jnp.where(row_a, outs[0], outs[1]).T.astype(yc_ref.dtype)


def _na_bias_kernel(r_ref, o_ref):
    w = GRID_W
    lane = lax.broadcasted_iota(jnp.int32, (w, LANES), 1)
    k = lax.broadcasted_iota(jnp.int32, (w, LANES), 0)
    col_start = jnp.clip(lane % w - NA_COLS // 2, 0, w - NA_COLS)
    col_ok = (k >= col_start) & (k < col_start + NA_COLS)
    left = lane < w
    lane_row = lax.broadcasted_iota(jnp.int32, (1, LANES), 1)
    neg = jnp.full((w, LANES), NEG, F32)
    n_dr = 2 * NA_ROWS - 1
    cache = {}

    def pair_tile(dr_a, ok_a, ok_b):
        key = (dr_a, ok_a, ok_b)
        if key not in cache:
            if not (ok_a or ok_b):
                cache[key] = neg
            else:
                ia = min(max(dr_a, 0), n_dr - 1)
                ib = min(max(dr_a - 1, 0), n_dr - 1)
                src = jnp.where(lane_row < w, r_ref[ia:ia + 1, :], pltpu.roll(r_ref[ib:ib + 1, :], w, axis=1))
                toep = pltpu.roll(jnp.broadcast_to(src, (w, LANES)), LANES - (NA_COLS - 1), axis=1,
                                  stride=1, stride_axis=0)
                ok = col_ok
                if not ok_a:
                    ok = ok & jnp.logical_not(left)
                if not ok_b:
                    ok = ok & left
                cache[key] = jnp.where(ok, toep * LOG2E, neg)
        return cache[key]

    offs = (0, NA_ROWS // 2, NA_ROWS)
    for v in range(3):
        for j in range(NA_WIN):
            for ii in range(NA_QROWS // 2):
                i = 2 * ii
                lo_a, lo_b = ((0, 0), (i, i + 1), (NA_QROWS, NA_QROWS))[v]
                dr_a = j - i + (NA_ROWS - 1) - offs[v]
                ok_a = lo_a <= j < lo_a + NA_ROWS
                ok_b = lo_b <= j < lo_b + NA_ROWS
                o_ref[v, j * w:(j + 1) * w, ii * LANES:(ii + 1) * LANES] = pair_tile(dr_a, ok_a, ok_b)


def _na_bias_table(rpb):
    depth, h, nr, nc = rpb.shape
    assert nr == 2 * NA_ROWS - 1 and nc == 2 * NA_COLS - 1 and 2 * GRID_W == LANES
    r_pad = jnp.pad(rpb.astype(F32)[..., ::-1], ((0, 0), (0, 0), (0, 16 - nr), (0, LANES - nc)))
    qn, kn = NA_QROWS * GRID_W, NA_WIN * GRID_W
    return pl.pallas_call(
        _na_bias_kernel,
        grid=(depth, h),
        in_specs=[pl.BlockSpec((None, None, 16, LANES), lambda l, hh: (l, hh, 0, 0))],
        out_specs=pl.BlockSpec((None, 3, None, kn, qn), lambda l, hh: (l, 0, hh, 0, 0)),
        out_shape=jax.ShapeDtypeStruct((depth, 3, h, kn, qn), F32),
        compiler_params=pltpu.CompilerParams(
            dimension_semantics=("arbitrary", "arbitrary"), vmem_limit_bytes=VMEM_LIMIT),
        name="natten_bias",
    )(r_pad)


def _na(ptx, pfx, ptc, pfc, btab, *, need_ctx, d_grp, k_group, q_group, v_group):
    b, t, _ = ptx.shape
    ct = ptc.shape[1]
    n_pairs = d_grp // LANES
    rows = t // GRID_W
    qn, kn = NA_QROWS * GRID_W, NA_WIN * GRID_W + ct
    assert (rows // NA_QROWS) % 2 == 0 and rows // NA_QROWS >= 4
    assert ct % qn == 0 and ct % NA_KEY_GROUP == 0 and kn % NA_KEY_GROUP == 0 and qn % MLSTM_L == 0

    def seq_specs(tlen):
        nc = tlen // MLSTM_L
        return [
            pl.BlockSpec((None, nc, LANES, MLSTM_L), lambda bi, hp: (bi, 0, q_group + hp, 0)),
            pl.BlockSpec((None, nc, LANES, MLSTM_L), lambda bi, hp: (bi, 0, v_group + hp, 0)),
            pl.BlockSpec((None, tlen, LANES), lambda bi, hp: (bi, 0, k_group + hp)),
        ]

    in_specs = seq_specs(t) + seq_specs(ct) + [
        pl.BlockSpec((3, 2) + btab.shape[2:], lambda bi, hp: (0, hp, 0, 0))]
    out_specs = [pl.BlockSpec((None, t, LANES), lambda bi, hp: (bi, 0, hp))]
    out_shape = [jax.ShapeDtypeStruct((b, t, d_grp), BF16)]
    if need_ctx:
        out_specs.append(pl.BlockSpec((None, ct, LANES), lambda bi, hp: (bi, 0, hp)))
        out_shape.append(jax.ShapeDtypeStruct((b, ct, d_grp), BF16))
    kern = functools.partial(_na_kernel, need_ctx=need_ctx, rows=rows)
    res = pl.pallas_call(
        kern,
        grid=(b, n_pairs),
        in_specs=in_specs,
        out_specs=out_specs,
        out_shape=out_shape,
        scratch_shapes=[
            pltpu.VMEM((2, 2, kn, qn), F32),
            pltpu.VMEM((2, 2, kn, qn), BF16),
            pltpu.VMEM((2, 2, 8, qn), F32),
        ],
        compiler_params=pltpu.CompilerParams(
            dimension_semantics=("arbitrary", "arbitrary"), vmem_limit_bytes=VMEM_LIMIT),
        name="natten_ctx" if need_ctx else "natten",
    )(pfx, pfx, ptx, pfc, pfc, ptc, btab)
    return (res[0], res[1]) if need_ctx else (res[0], None)


def _out_mlp_kernel(x_ref, ym_ref, yn_ref, mod_ref, g_ref, wo_ref, w1_ref, w2_ref, fg_ref, o_ref,
                    *, d_model, final_norm):
    d = d_model
    dm = ym_ref.shape[1]
    att = _dot(ym_ref[...], wo_ref[0:dm, :]) + _dot(yn_ref[...], wo_ref[dm:, :])
    x1 = x_ref[...] + mod_ref[:, 2 * d:3 * d] * att
    h = _modulated_norm(x1, g_ref[...], mod_ref[:, 4 * d:5 * d], mod_ref[:, 3 * d:4 * d]).astype(BF16)
    d_ff = w1_ref.shape[1]
    acc = None
    for c in range(d_ff // COL_GROUP):
        hid = _dot(h, w1_ref[:, c * COL_GROUP:(c + 1) * COL_GROUP])
        hid = jnp.square(jnp.maximum(hid, 0.0)).astype(BF16)
        part = _dot(hid, w2_ref[c * COL_GROUP:(c + 1) * COL_GROUP, :])
        acc = part if acc is None else acc + part
    x2 = x1 + mod_ref[:, 5 * d:6 * d] * acc
    if final_norm:
        x2 = (x2 * lax.rsqrt(jnp.mean(x2 * x2, axis=-1, keepdims=True) + EPS)) * fg_ref[...]
    o_ref[...] = x2


def _out_mlp(x, ym, yn, mod4, mod_row, g, w_out, w1, w2, final_g, *, final_norm):
    b, t, d = x.shape
    dm = ym.shape[2]
    d_ff = w1.shape[1]
    tm = min(TOKEN_TILE, t)
    kern = functools.partial(_out_mlp_kernel, d_model=d, final_norm=final_norm)
    const = lambda i, bi: (0, 0)
    return pl.pallas_call(
        kern,
        grid=(t // tm, b),
        in_specs=[
            pl.BlockSpec((None, tm, d), lambda i, bi: (bi, i, 0)),
            pl.BlockSpec((None, tm, dm), lambda i, bi: (bi, i, 0)),
            pl.BlockSpec((None, tm, dm), lambda i, bi: (bi, i, 0)),
            pl.BlockSpec((None, 1, N_MOD * d), lambda i, bi: (mod_row(bi), 0, 0)),
            pl.BlockSpec((1, d), const),
            pl.BlockSpec((d, d), const, pipeline_mode=pl.Buffered(1)),
            pl.BlockSpec((d, d_ff), const, pipeline_mode=pl.Buffered(1)),
            pl.BlockSpec((d_ff, d), const, pipeline_mode=pl.Buffered(1)),
            pl.BlockSpec((1, d), const),
        ],
        out_specs=pl.BlockSpec((None, tm, d), lambda i, bi: (bi, i, 0)),
        out_shape=jax.ShapeDtypeStruct((b, t, d), F32),
        compiler_params=pltpu.CompilerParams(
            dimension_semantics=("arbitrary", "arbitrary"), vmem_limit_bytes=VMEM_LIMIT),
        name="out_mlp_final" if final_norm else "out_mlp",
    )(x, ym, yn, mod4, g, w_out, w1, w2, final_g)


def _rope_tables(n_tokens):
    t = jnp.arange(n_tokens)
    row = (t // GRID_W).astype(F32)
    col = (t % GRID_W).astype(F32)
    n_freq = HEAD_DIM // 4
    inv_freq = ROPE_BASE ** (-jnp.arange(n_freq, dtype=F32) / n_freq)
    ang = jnp.concatenate([row[:, None] * inv_freq, col[:, None] * inv_freq], axis=-1)
    cos, sin = jnp.cos(ang), jnp.sin(ang)
    reps = LANES // HEAD_DIM
    return (jnp.tile(jnp.concatenate([cos, cos], axis=-1), (1, reps)),
            jnp.tile(jnp.concatenate([-sin, sin], axis=-1), (1, reps)))


def kernel(x, c, ctx, c_ctx, w_ada, b_ada, norm1_g, w_in, b_gate, mlstm_norm_g, rpb, w_out, norm2_g,
           w_mlp1, w_mlp2, final_g):
    b, s, d = x.shape
    ct = ctx.shape[1]
    depth = w_ada.shape[0]
    d_grp = d // 2
    nh = d_grp // HEAD_DIM
    n_gate = b_gate.shape[1]
    n_main = w_in.shape[2] - n_gate
    assert n_main == 7 * d_grp and d_grp % LANES == 0 and n_gate == 4 * nh and n_gate <= LANES
    assert s % TOKEN_TILE == 0 and ct % MLSTM_L == 0 and (s // GRID_W) % NA_QROWS == 0

    n_rows = -(-(b + 1) // 8) * 8
    cc = jnp.zeros((n_rows, d), F32).at[:b].set(c).at[b].set(c_ctx)
    mod = _adaln(cc, w_ada, b_ada)
    mod = mod.reshape(depth, n_rows, 1, N_MOD * d)

    cos_t, sin_t = _rope_tables(s)
    grp = lambda k: w_in[:, :, k * d_grp:(k + 1) * d_grp]
    w_tok = jnp.concatenate([grp(1), grp(3), grp(5)], axis=2).astype(BF16)
    w_feat = jnp.swapaxes(jnp.concatenate([grp(0), grp(2), grp(4), grp(6)], axis=2), 1, 2).astype(BF16)
    w_gate = jnp.pad(w_in[:, :, n_main:], ((0, 0), (0, 0), (0, LANES - n_gate))).astype(BF16)
    b_gate_p = jnp.pad(b_gate.astype(F32), ((0, 0), (0, LANES - n_gate))).reshape(depth, 1, LANES)
    w_out_b = w_out.astype(BF16)
    w1_b = w_mlp1.astype(BF16)
    w2_b = w_mlp2.astype(BF16)
    n_pairs = d_grp // LANES
    btab = _na_bias_table(rpb)

    x_row = lambda bi: bi
    c_row = lambda bi: b
    xc = ctx
    for l in range(depth):
        need_ctx = l < depth - 1
        g1 = norm1_g[l].reshape(1, d)
        g2 = norm2_g[l].reshape(1, d)
        ptx, pfx, gcx, grx = _in_proj(x, mod[l], x_row, g1, w_tok[l], w_feat[l], w_gate[l], b_gate_p[l],
                                      cos_t, sin_t, rope=True, nh=nh)
        ptc, pfc, gcc, grc = _in_proj(xc, mod[l], c_row, g1, w_tok[l], w_feat[l], w_gate[l], b_gate_p[l],
                                      cos_t[:ct], sin_t[:ct], rope=False, nh=nh)
        ym, ycm = _mlstm(ptx, pfx, gcx, grx, ptc, pfc, gcc, grc, mlstm_norm_g[l].reshape(1, d_grp),
                         need_ctx=need_ctx, d_grp=d_grp)
        yn, ycn = _na(ptx, pfx, ptc, pfc, btab[l], need_ctx=need_ctx, d_grp=d_grp,
                      k_group=2 * n_pairs, q_group=2 * n_pairs, v_group=3 * n_pairs)
        fg = final_g.reshape(1, d)
        x = _out_mlp(x, ym, yn, mod[l], x_row, g2, w_out_b[l], w1_b[l], w2_b[l], fg,
                     final_norm=(l == depth - 1))
        if need_ctx:
            xc = _out_mlp(xc, ycm, ycn, mod[l], c_row, g2, w_out_b[l], w1_b[l], w2_b[l], fg,
                          final_norm=False)
    return x
```

```python
import functools

import jax
import jax.numpy as jnp
import numpy as np
from jax import lax
from jax.experimental import pallas as pl
from jax.experimental.pallas import tpu as pltpu

HEAD_DIM = 64
GRID_W = 64
NA_ROWS = 8
NA_COLS = 16
N_MOD = 6
ROPE_BASE = 10000.0
EPS = 1e-6

LANES = 128
MLSTM_L = 128
NA_QROWS = 4
NA_WIN = NA_QROWS + NA_ROWS
NEG = -1e30
LOG2E = 1.4426950408889634
COL_GROUP = 512
TOKEN_TILE = 512
VMEM_LIMIT = 56 * 1024 * 1024

F32 = jnp.float32
BF16 = jnp.bfloat16


def _dot(a, b):
    return jnp.dot(a, b, preferred_element_type=F32)


def _dot_nt(a, b):
    return lax.dot_general(a, b, (((1,), (1,)), ((), ())), preferred_element_type=F32)


def _split3(a):
    p0 = a.astype(BF16)
    r1 = a - p0.astype(F32)
    p1 = r1.astype(BF16)
    p2 = (r1 - p1.astype(F32)).astype(BF16)
    return p0, p1, p2


def _sigmoid(v):
    return 1.0 / (1.0 + jnp.exp(-v))


def _log_sigmoid(v):
    return jnp.minimum(v, 0.0) - jnp.log(1.0 + jnp.exp(-jnp.abs(v)))


def _adaln_kernel(c_ref, w_ref, b_ref, o_ref):
    c = c_ref[...]
    s = (c * _sigmoid(c)).astype(BF16)
    o_ref[...] = _dot(s, w_ref[...].astype(BF16)) + b_ref[...]


def _adaln(cc, w_ada, b_ada):
    depth, d, n = w_ada.shape
    rows = cc.shape[0]
    tn = n // 4
    return pl.pallas_call(
        _adaln_kernel,
        grid=(depth, n // tn),
        in_specs=[
            pl.BlockSpec((rows, d), lambda l, j: (0, 0)),
            pl.BlockSpec((None, d, tn), lambda l, j: (l, 0, j)),
            pl.BlockSpec((None, 1, tn), lambda l, j: (l, 0, j)),
        ],
        out_specs=pl.BlockSpec((None, rows, tn), lambda l, j: (l, 0, j)),
        out_shape=jax.ShapeDtypeStruct((depth, rows, n), F32),
        compiler_params=pltpu.CompilerParams(
            dimension_semantics=("arbitrary", "arbitrary"), vmem_limit_bytes=VMEM_LIMIT),
        name="adaln",
    )(cc, w_ada, b_ada.reshape(depth, 1, n))


def _modulated_norm(x, g, scale, shift):
    y = x * lax.rsqrt(jnp.mean(x * x, axis=-1, keepdims=True) + EPS)
    return (y * g) * (1.0 + scale) + shift


def _in_proj_gates(g, gc_ref, gr_ref, nh):
    L = MLSTM_L
    lf = _log_sigmoid(g[2 * nh:4 * nh])
    u_ = lax.broadcasted_iota(jnp.int32, (L, L), 0)
    s_ = lax.broadcasted_iota(jnp.int32, (L, L), 1)
    tri_prefix = (u_ <= s_).astype(BF16)
    fwd_rows = lax.broadcasted_iota(jnp.int32, (2 * nh, L), 0) < nh
    pad = jnp.zeros((LANES - 8 * nh, L), F32)
    for c in range(g.shape[1] // L):
        lfc = lf[:, c * L:(c + 1) * L]
        b_pre = sum(_dot(p, tri_prefix) for p in _split3(lfc))
        tot = jnp.broadcast_to(jnp.sum(lfc, axis=1, keepdims=True), (2 * nh, L))
        b = jnp.where(fwd_rows, b_pre, tot - b_pre + lfc)
        a = g[0:2 * nh, c * L:(c + 1) * L] - b
        amax = jnp.broadcast_to(jnp.max(a, axis=1, keepdims=True), (2 * nh, L))
        out = jnp.concatenate([a, b, tot, amax], axis=0) * LOG2E
        gr_ref[c] = out
        gc_ref[c * L:(c + 1) * L, :] = jnp.concatenate([out, pad], axis=0).T


def _in_proj_kernel(x_ref, mod_ref, g_ref, wt_ref, wf_ref, bg_ref, cos_ref, sin_ref,
                    pt_ref, pf_ref, gc_ref, gr_ref, *, d_model, rope, nh):
    d = d_model
    tm = x_ref.shape[0]
    L = MLSTM_L
    n_chunks = tm // L
    h = _modulated_norm(x_ref[...], g_ref[...], mod_ref[:, d:2 * d], mod_ref[:, 0:d])
    hb = h.astype(BF16)
    q_scale = HEAD_DIM ** -0.5
    half = HEAD_DIM // 2

    lane = lax.broadcasted_iota(jnp.int32, (tm, LANES), 1)
    first_half = (lane % HEAD_DIM) < half
    for cg in range(wt_ref.shape[1] // COL_GROUP):
        acc = _dot(hb, wt_ref[:, cg * COL_GROUP:(cg + 1) * COL_GROUP])
        for s in range(COL_GROUP // LANES):
            slab = acc[:, s * LANES:(s + 1) * LANES]
            if rope and cg == 0:
                partner = jnp.where(first_half,
                                    pltpu.roll(slab, LANES - half, axis=1),
                                    pltpu.roll(slab, half, axis=1))
                slab = slab * cos_ref[...] + partner * sin_ref[...]
            c0 = cg * COL_GROUP + s * LANES
            pt_ref[:, c0:c0 + LANES] = slab.astype(BF16)

    d_grp = COL_GROUP
    part_scale = {0: q_scale, 2: q_scale * LOG2E}
    if rope:
        cos_t = cos_ref[...].T[0:half]
        sin_t = sin_ref[...].T[half:HEAD_DIM]
    n_parts = wf_ref.shape[0] // d_grp
    n_gate = wf_ref.shape[0] - n_parts * d_grp
    for part in range(n_parts):
        last = part == n_parts - 1
        acc = _dot_nt(wf_ref[part * d_grp:(part + 1) * d_grp + (n_gate if last else 0), :], hb)
        if last:
            _in_proj_gates(acc[d_grp:] + bg_ref[:, 0:1], gc_ref, gr_ref, nh)
        for hh in range(d_grp // HEAD_DIM):
            r0 = hh * HEAD_DIM
            if part == 0 and rope:
                x1 = acc[r0:r0 + half]
                x2 = acc[r0 + half:r0 + HEAD_DIM]
                blk = jnp.concatenate([x1 * cos_t - x2 * sin_t, x1 * sin_t + x2 * cos_t], axis=0)
            else:
                blk = acc[r0:r0 + HEAD_DIM]
            if part in part_scale:
                blk = blk * part_scale[part]
            blk = blk.astype(BF16)
            for c in range(n_chunks):
                pf_ref[c, part * d_grp + r0:part * d_grp + r0 + HEAD_DIM, :] = blk[:, c * L:(c + 1) * L]


def _in_proj(x, mod4, mod_row, g, w_tok, w_feat, b_gate, cos_t, sin_t, *, rope, nh):
    b, t, d = x.shape
    nt = w_tok.shape[1]
    nfw = w_feat.shape[0]
    nf = nfw - 4 * nh
    tm = min(TOKEN_TILE, t)
    nc = tm // MLSTM_L
    kern = functools.partial(_in_proj_kernel, d_model=d, rope=rope, nh=nh)
    const = lambda i, bi: (0, 0)
    return pl.pallas_call(
        kern,
        grid=(t // tm, b),
        in_specs=[
            pl.BlockSpec((None, tm, d), lambda i, bi: (bi, i, 0)),
            pl.BlockSpec((None, 1, N_MOD * d), lambda i, bi: (mod_row(bi), 0, 0)),
            pl.BlockSpec((1, d), const),
            pl.BlockSpec((d, nt), const, pipeline_mode=pl.Buffered(1)),
            pl.BlockSpec((nfw, d), const, pipeline_mode=pl.Buffered(1)),
            pl.BlockSpec((4 * nh, LANES), const),
            pl.BlockSpec((tm, LANES), lambda i, bi: (i, 0)),
            pl.BlockSpec((tm, LANES), lambda i, bi: (i, 0)),
        ],
        out_specs=[
            pl.BlockSpec((None, tm, nt), lambda i, bi: (bi, i, 0)),
            pl.BlockSpec((None, nc, nf, MLSTM_L), lambda i, bi: (bi, i, 0, 0)),
            pl.BlockSpec((None, tm, LANES), lambda i, bi: (bi, i, 0)),
            pl.BlockSpec((None, nc, 8 * nh, MLSTM_L), lambda i, bi: (bi, i, 0, 0)),
        ],
        out_shape=[
            jax.ShapeDtypeStruct((b, t, nt), BF16),
            jax.ShapeDtypeStruct((b, t // MLSTM_L, nf, MLSTM_L), BF16),
            jax.ShapeDtypeStruct((b, t, LANES), F32),
            jax.ShapeDtypeStruct((b, t // MLSTM_L, 8 * nh, MLSTM_L), F32),
        ],
        compiler_params=pltpu.CompilerParams(
            dimension_semantics=("arbitrary", "arbitrary"), vmem_limit_bytes=VMEM_LIMIT),
        name="in_proj_rope" if rope else "in_proj",
    )(x, mod4, g, w_tok, w_feat, b_gate, cos_t, sin_t)


def _chunk_rows(j):
    if isinstance(j, int):
        return pl.ds(j * MLSTM_L, MLSTM_L)
    return pl.ds(pl.multiple_of(j * MLSTM_L, MLSTM_L), MLSTM_L)


N_REP = 16
AUG = HEAD_DIM + N_REP


def _mlstm_intra(qt_ref, vt_ref, k_ref, gc_ref, gr_ref, j, tau, hp, m_state, p_ref, rv_ref, u_ref,
                 *, rev, nh):
    L = MLSTM_L
    d = HEAD_DIM
    rows = _chunk_rows(j)
    gcb = gc_ref[rows, :]
    qt = qt_ref[j]
    k2 = k_ref[rows, :]
    lane = lax.broadcasted_iota(jnp.int32, (L, LANES), 1)
    sub = lax.broadcasted_iota(jnp.int32, (L, LANES), 0)
    tri = (sub >= lane) if rev else (sub <= lane)
    zero = jnp.zeros_like(qt)
    qbd = jnp.concatenate([jnp.where(sub < d, qt, zero), jnp.where(sub >= d, qt, zero)], axis=1)
    st = _dot(k2, qbd)
    base = nh if rev else 0
    m_new, ws = [], []
    for hd in range(2):
        r = base + 2 * hp + hd
        a_col = jnp.sum(jnp.where(lane == r, gcb, 0.0), axis=1, keepdims=True)
        a_row = gr_ref[j, pl.ds(r, 1), :]
        b_row = gr_ref[j, pl.ds(2 * nh + r, 1), :]
        tot = gr_ref[j, pl.ds(4 * nh + r, 1), :]
        amax = gr_ref[j, pl.ds(6 * nh + r, 1), :]
        m_h = m_state[hd]
        at = jnp.where(tri, a_col, NEG)
        mrow = jnp.maximum(jnp.max(at, axis=0, keepdims=True), m_h)
        p_ref[int(rev), tau, :, hd * L:(hd + 1) * L] = (
            st[:, hd * L:(hd + 1) * L] * jnp.exp2(at - mrow)).astype(BF16)
        m_end = jnp.maximum(m_h, amax)
        ws.append(jnp.exp2(a_row - m_end))
        rv_ref[int(rev), tau, 4 * hd:4 * hd + 4, :] = jnp.concatenate([
            jnp.exp2(m_h - mrow),
            jnp.exp2(-(b_row + mrow)),
            ws[hd],
            jnp.exp2(m_h - m_end),
        ], axis=0)
        m_new.append(tot + m_end)
    vf = vt_ref[j].astype(F32)
    vsc = jnp.concatenate(
        [(vf[hd * d:(hd + 1) * d] * ws[hd]).astype(BF16) for hd in range(2)]
        + [jnp.broadcast_to(ws[hd], (N_REP, L)).astype(BF16) for hd in range(2)], axis=0)
    u_ref[int(rev), tau] = jnp.where(_own_features(), _dot(vsc, k2), 0.0)
    return m_new


def _head_a_rows():
    row = lax.broadcasted_iota(jnp.int32, (2 * AUG, LANES), 0)
    return (row < HEAD_DIM) | ((row >= 2 * HEAD_DIM) & (row < 2 * HEAD_DIM + N_REP))


def _own_features():
    lane = lax.broadcasted_iota(jnp.int32, (2 * AUG, LANES), 1)
    return _head_a_rows() == (lane < HEAD_DIM)


def _mlstm_chunk(qt_ref, vt_ref, j, tau, ct_ref, p_ref, rv_ref, u_ref, hacc_ref, *, rev):
    L = MLSTM_L
    d = HEAD_DIM
    qt = qt_ref[j]
    vt = vt_ref[j]
    rv = rv_ref[int(rev), tau]
    wprev, eneg, _, wc = ([rv[4 * hd + i:4 * hd + i + 1] for hd in range(2)] for i in range(4))
    qf = qt.astype(F32)
    qw = jnp.concatenate([(qf * wprev[0]).astype(BF16), (qf * wprev[1]).astype(BF16)], axis=1)
    rhs = jnp.concatenate([p_ref[int(rev), tau], qw], axis=0)
    ones = jnp.ones((2 * N_REP, L), BF16)
    lhs = jnp.concatenate([jnp.concatenate([vt, ones], axis=0), ct_ref[...].astype(BF16)], axis=1)
    res = _dot(lhs, rhs)
    hs = []
    for hd in range(2):
        num = res[hd * d:(hd + 1) * d, hd * L:(hd + 1) * L]
        den = res[2 * d + hd * N_REP:2 * d + hd * N_REP + 1, hd * L:(hd + 1) * L]
        hs.append(num * (1.0 / jnp.maximum(jnp.abs(den), eneg[hd])))
    hacc_ref[int(rev), j] = jnp.concatenate(hs, axis=0)
    ct_ref[...] = jnp.where(_head_a_rows(), wc[0], wc[1]) * ct_ref[...] + u_ref[int(rev), tau]


def _mlstm_finish(hacc_ref, o_ref, g_ref, y_ref, n_chunks):
    sub = lax.broadcasted_iota(jnp.int32, (LANES, MLSTM_L), 0)
    row_a = sub < HEAD_DIM

    def head_mean(v):
        ma = jnp.mean(v[:HEAD_DIM], axis=0, keepdims=True)
        mb = jnp.mean(v[HEAD_DIM:], axis=0, keepdims=True)
        return jnp.where(row_a, ma, mb)

    def body(j, carry):
        rows = _chunk_rows(j)
        hs = hacc_ref[0, j] + hacc_ref[1, j]
        dv = hs - head_mean(hs)
        yn = (dv * lax.rsqrt(head_mean(dv * dv) + EPS)).T
        gate = _sigmoid(o_ref[rows, :].astype(F32))
        y_ref[rows, :] = (yn * g_ref[...] * gate).astype(y_ref.dtype)
        return carry

    lax.fori_loop(0, n_chunks, body, 0, unroll=min(4, n_chunks))


def _mlstm_kernel(qtx_ref, vtx_ref, kx_ref, ox_ref, gcx_ref, grx_ref,
                  qtc_ref, vtc_ref, kc_ref, oc_ref, gcc_ref, grc_ref, g_ref, *rest, need_ctx, nh):
    if need_ctx:
        y_ref, yc_ref, cf_ref, cb_ref, p_ref, rv_ref, u_ref, hx_ref, hc_ref = rest
    else:
        y_ref, cf_ref, cb_ref, p_ref, rv_ref, u_ref, hx_ref, hc_ref = rest
        yc_ref = None
    hp = pl.program_id(1)
    ncx = qtx_ref.shape[0]
    ncc = qtc_ref.shape[0]
    intra = functools.partial(_mlstm_intra, hp=hp, p_ref=p_ref, rv_ref=rv_ref, u_ref=u_ref, nh=nh)
    zero = jnp.zeros((1, MLSTM_L), F32)
    mf = [zero, zero]
    mb = [zero, zero]
    for j in range(ncc):
        mf = intra(qtc_ref, vtc_ref, kc_ref, gcc_ref, grc_ref, j, j, m_state=mf, rev=False)
        mb = intra(qtc_ref, vtc_ref, kc_ref, gcc_ref, grc_ref, ncc - 1 - j, j, m_state=mb, rev=True)

    def intra_body(j, carry):
        mf_ = intra(qtx_ref, vtx_ref, kx_ref, gcx_ref, grx_ref, j, ncc + j,
                    m_state=list(carry[0:2]), rev=False)
        mb_ = intra(qtx_ref, vtx_ref, kx_ref, gcx_ref, grx_ref, ncx - 1 - j, ncc + j,
                    m_state=list(carry[2:4]), rev=True)
        return (mf_[0], mf_[1], mb_[0], mb_[1])

    lax.fori_loop(0, ncx, intra_body, (mf[0], mf[1], mb[0], mb[1]), unroll=2)

    cf_ref[...] = jnp.zeros_like(cf_ref)
    cb_ref[...] = jnp.zeros_like(cb_ref)
    step = functools.partial(_mlstm_chunk, p_ref=p_ref, rv_ref=rv_ref, u_ref=u_ref)
    for j in range(ncc):
        step(qtc_ref, vtc_ref, j, j, cf_ref, hacc_ref=hc_ref, rev=False)
        step(qtc_ref, vtc_ref, ncc - 1 - j, j, cb_ref, hacc_ref=hc_ref, rev=True)

    def body(j, carry):
        step(qtx_ref, vtx_ref, j, ncc + j, cf_ref, hacc_ref=hx_ref, rev=False)
        step(qtx_ref, vtx_ref, ncx - 1 - j, ncc + j, cb_ref, hacc_ref=hx_ref, rev=True)
        return carry

    lax.fori_loop(0, ncx, body, 0, unroll=4)
    _mlstm_finish(hx_ref, ox_ref, g_ref, y_ref, ncx)
    if need_ctx:
        _mlstm_finish(hc_ref, oc_ref, g_ref, yc_ref, ncc)


def _mlstm(ptx, pfx, gcx, grx, ptc, pfc, gcc, grc, norm_g, *, need_ctx, d_grp):
    b, t, _ = ptx.shape
    ct = ptc.shape[1]
    nh = d_grp // HEAD_DIM
    n_pairs = d_grp // LANES
    n_steps = (t + ct) // MLSTM_L

    def seq_specs(tlen, pf, gr):
        nc = tlen // MLSTM_L
        return [
            pl.BlockSpec((None, nc, LANES, MLSTM_L), lambda bi, hp: (bi, 0, hp, 0)),
            pl.BlockSpec((None, nc, LANES, MLSTM_L), lambda bi, hp: (bi, 0, n_pairs + hp, 0)),
            pl.BlockSpec((None, tlen, LANES), lambda bi, hp: (bi, 0, hp)),
            pl.BlockSpec((None, tlen, LANES), lambda bi, hp: (bi, 0, n_pairs + hp)),
            pl.BlockSpec((None, tlen, LANES), lambda bi, hp: (bi, 0, 0)),
            pl.BlockSpec((None,) + gr.shape[1:], lambda bi, hp: (bi, 0, 0, 0)),
        ]

    out_specs = [pl.BlockSpec((None, t, LANES), lambda bi, hp: (bi, 0, hp))]
    out_shape = [jax.ShapeDtypeStruct((b, t, d_grp), BF16)]
    if need_ctx:
        out_specs.append(pl.BlockSpec((None, ct, LANES), lambda bi, hp: (bi, 0, hp)))
        out_shape.append(jax.ShapeDtypeStruct((b, ct, d_grp), BF16))
    kern = functools.partial(_mlstm_kernel, need_ctx=need_ctx, nh=nh)
    res = pl.pallas_call(
        kern,
        grid=(b, n_pairs),
        in_specs=seq_specs(t, pfx, grx) + seq_specs(ct, pfc, grc) + [
            pl.BlockSpec((1, LANES), lambda bi, hp: (0, hp))],
        out_specs=out_specs,
        out_shape=out_shape,
        scratch_shapes=[
            pltpu.VMEM((2 * AUG, LANES), F32),
            pltpu.VMEM((2 * AUG, LANES), F32),
            pltpu.VMEM((2, n_steps, MLSTM_L, 2 * MLSTM_L), BF16),
            pltpu.VMEM((2, n_steps, 8, MLSTM_L), F32),
            pltpu.VMEM((2, n_steps, 2 * AUG, LANES), F32),
            pltpu.VMEM((2, t // MLSTM_L, LANES, MLSTM_L), F32),
            pltpu.VMEM((2, ct // MLSTM_L, LANES, MLSTM_L), F32),
        ],
        compiler_params=pltpu.CompilerParams(
            dimension_semantics=("arbitrary", "arbitrary"), vmem_limit_bytes=VMEM_LIMIT),
        name="mlstm_ctx" if need_ctx else "mlstm",
    )(pfx, pfx, ptx, ptx, gcx, grx, pfc, pfc, ptc, ptc, gcc, grc, norm_g)
    return (res[0], res[1]) if need_ctx else (res[0], None)


NA_KEY_GROUP = 256


def _na_softmax(s_ref, p_ref, idx, n_keys, n_q, col_max):
    sums = []
    for lt in range(n_q // LANES):
        cols = slice(lt * LANES, (lt + 1) * LANES)
        m = col_max[:, cols]
        tot = None
        for kg in range(n_keys // NA_KEY_GROUP):
            r = slice(kg * NA_KEY_GROUP, (kg + 1) * NA_KEY_GROUP)
            p = jnp.exp2(s_ref[idx + (r, cols)] - m)
            p_ref[idx + (r, cols)] = p.astype(BF16)
            sk = jnp.sum(p, axis=0, keepdims=True)
            tot = sk if tot is None else tot + sk
        sums.append(tot)
    return jnp.concatenate(sums, axis=1)


def _na_kernel(qt_ref, vt_ref, k_ref, qtc_ref, vtc_ref, kc_ref, bt_ref, *rest, need_ctx, rows):
    if need_ctx:
        y_ref, yc_ref, s_ref, p_ref, m_ref, l_ref = rest
    else:
        y_ref, s_ref, p_ref, m_ref, l_ref = rest
        yc_ref = None
    w = GRID_W
    L = MLSTM_L
    qn = NA_QROWS * w
    kn = NA_WIN * w
    n_ctx = kc_ref.shape[0]
    nblk = rows // NA_QROWS
    kc = kc_ref[...]
    vtc = jnp.concatenate([vtc_ref[c] for c in range(n_ctx // L)], axis=1)
    sub = lax.broadcasted_iota(jnp.int32, (LANES, qn), 0)
    row_a = sub < HEAD_DIM

    def first_key_row(ib):
        if isinstance(ib, int):
            return min(max(ib * NA_QROWS - NA_ROWS // 2, 0), rows - NA_WIN)
        return jnp.clip(ib * NA_QROWS - NA_ROWS // 2, 0, rows - NA_WIN)

    def token_rows(start, n):
        if isinstance(start, int):
            return pl.ds(start, n)
        return pl.ds(pl.multiple_of(start, w), n)

    def chunks(ref, tok0, n_tok):
        c0 = tok0 // L
        return jnp.concatenate([ref[c0 + c] for c in range(n_tok // L)], axis=1)

    def head_queries(qt, hd):
        return jnp.where(row_a if hd == 0 else jnp.logical_not(row_a), qt, jnp.zeros_like(qt))

    def scores(ib, buf):
        if isinstance(ib, int):
            var = 0 if ib == 0 else (2 if ib == nblk - 1 else 1)
        else:
            var = jnp.where(ib == nblk - 1, 2, 1)
        qt = chunks(qt_ref, ib * qn, qn)
        kw = k_ref[token_rows(first_key_row(ib) * w, kn), :]
        for hd in range(2):
            qh = head_queries(qt, hd)
            s_loc = _dot(kw, qh) + bt_ref[var, hd]
            s_ctx = _dot(kc, qh)
            s_ref[buf, hd, 0:kn, :] = s_loc
            s_ref[buf, hd, kn:, :] = s_ctx
            m_ref[buf, hd] = jnp.broadcast_to(
                jnp.maximum(jnp.max(s_loc, axis=0, keepdims=True), jnp.max(s_ctx, axis=0, keepdims=True)),
                (8, qn))

    def softmax(buf):
        for hd in range(2):
            l_ref[buf, hd] = jnp.broadcast_to(
                _na_softmax(s_ref, p_ref, (buf, hd), kn + n_ctx, qn, m_ref[buf, hd, 0:1, :]), (8, qn))

    def values(ib, buf):
        vt = jnp.concatenate([chunks(vt_ref, first_key_row(ib) * w, kn), vtc], axis=1)
        outs = [_dot(vt, p_ref[buf, hd]) * (1.0 / l_ref[buf, hd, 0:1, :]) for hd in range(2)]
        y_ref[token_rows(ib * qn, qn), :] = jnp.where(row_a, outs[0], outs[1]).T.astype(y_ref.dtype)

    scores(0, 0)
    scores(1, 1)
    softmax(0)

    def body(tt, carry):
        t = 2 * tt + 1
        scores(t + 1, 0)
        softmax(1)
        values(t - 1, 0)
        scores(t + 2, 1)
        softmax(0)
        values(t, 1)
        return carry

    lax.fori_loop(0, (nblk - 2) // 2, body, 0)
    softmax(1)
    values(nblk - 2, 0)
    values(nblk - 1, 1)

    if need_ctx:
        qtc = jnp.concatenate([qtc_ref[c] for c in range(n_ctx // L)], axis=1)
        sub_c = lax.broadcasted_iota(jnp.int32, qtc.shape, 0)
        row_ac = sub_c < HEAD_DIM
        for c0 in range(0, n_ctx, qn):
            outs = []
            for hd in range(2):
                qh = jnp.where(row_ac if hd == 0 else jnp.logical_not(row_ac), qtc, jnp.zeros_like(qtc))
                s_cc = _dot(kc, qh[:, c0:c0 + qn])
                s_ref[0, hd, 0:n_ctx, :] = s_cc
                lsum = _na_softmax(s_ref, p_ref, (0, hd), n_ctx, qn, jnp.max(s_cc, axis=0, keepdims=True))
                outs.append(_dot(vtc, p_ref[0, hd, 0:n_ctx, :]) * (1.0 / lsum))
            yc_ref[c0:c0 + qn, :] = jnp.where(row_a, outs[0], outs[1]).T.astype(yc_ref.dtype)


def _na_bias_kernel(r_ref, o_ref):
    w = GRID_W
    lane = lax.broadcasted_iota(jnp.int32, (w, LANES), 1)
    k = lax.broadcasted_iota(jnp.int32, (w, LANES), 0)
    col_start = jnp.clip(lane % w - NA_COLS // 2, 0, w - NA_COLS)
    col_ok = (k >= col_start) & (k < col_start + NA_COLS)
    left = lane < w
    lane_row = lax.broadcasted_iota(jnp.int32, (1, LANES), 1)
    neg = jnp.full((w, LANES), NEG, F32)
    n_dr = 2 * NA_ROWS - 1
    cache = {}

    def pair_tile(dr_a, ok_a, ok_b):
        key = (dr_a, ok_a, ok_b)
        if key not in cache:
            if not (ok_a or ok_b):
                cache[key] = neg
            else:
                ia = min(max(dr_a, 0), n_dr - 1)
                ib = min(max(dr_a - 1, 0), n_dr - 1)
                src = jnp.where(lane_row < w, r_ref[ia:ia + 1, :], pltpu.roll(r_ref[ib:ib + 1, :], w, axis=1))
                toep = pltpu.roll(jnp.broadcast_to(src, (w, LANES)), LANES - (NA_COLS - 1), axis=1,
                                  stride=1, stride_axis=0)
                ok = col_ok
                if not ok_a:
                    ok = ok & jnp.logical_not(left)
                if not ok_b:
                    ok = ok & left
                cache[key] = jnp.where(ok, toep * LOG2E, neg)
        return cache[key]

    offs = (0, NA_ROWS // 2, NA_ROWS)
    for v in range(3):
        for j in range(NA_WIN):
            for ii in range(NA_QROWS // 2):
                i = 2 * ii
                lo_a, lo_b = ((0, 0), (i, i + 1), (NA_QROWS, NA_QROWS))[v]
                dr_a = j - i + (NA_ROWS - 1) - offs[v]
                ok_a = lo_a <= j < lo_a + NA_ROWS
                ok_b = lo_b <= j < lo_b + NA_ROWS
                o_ref[v, j * w:(j + 1) * w, ii * LANES:(ii + 1) * LANES] = pair_tile(dr_a, ok_a, ok_b)


def _na_bias_table(rpb):
    depth, h, nr, nc = rpb.shape
    assert nr == 2 * NA_ROWS - 1 and nc == 2 * NA_COLS - 1 and 2 * GRID_W == LANES
    r_pad = jnp.pad(rpb.astype(F32)[..., ::-1], ((0, 0), (0, 0), (0, 16 - nr), (0, LANES - nc)))
    qn, kn = NA_QROWS * GRID_W, NA_WIN * GRID_W
    return pl.pallas_call(
        _na_bias_kernel,
        grid=(depth, h),
        in_specs=[pl.BlockSpec((None, None, 16, LANES), lambda l, hh: (l, hh, 0, 0))],
        out_specs=pl.BlockSpec((None, 3, None, kn, qn), lambda l, hh: (l, 0, hh, 0, 0)),
        out_shape=jax.ShapeDtypeStruct((depth, 3, h, kn, qn), F32),
        compiler_params=pltpu.CompilerParams(
            dimension_semantics=("arbitrary", "arbitrary"), vmem_limit_bytes=VMEM_LIMIT),
        name="natten_bias",
    )(r_pad)


def _na(ptx, pfx, ptc, pfc, btab, *, need_ctx, d_grp, k_group, q_group, v_group):
    b, t, _ = ptx.shape
    ct = ptc.shape[1]
    n_pairs = d_grp // LANES
    rows = t // GRID_W
    qn, kn = NA_QROWS * GRID_W, NA_WIN * GRID_W + ct
    assert (rows // NA_QROWS) % 2 == 0 and rows // NA_QROWS >= 4
    assert ct % qn == 0 and ct % NA_KEY_GROUP == 0 and kn % NA_KEY_GROUP == 0 and qn % MLSTM_L == 0

    def seq_specs(tlen):
        nc = tlen // MLSTM_L
        return [
            pl.BlockSpec((None, nc, LANES, MLSTM_L), lambda bi, hp: (bi, 0, q_group + hp, 0)),
            pl.BlockSpec((None, nc, LANES, MLSTM_L), lambda bi, hp: (bi, 0, v_group + hp, 0)),
            pl.BlockSpec((None, tlen, LANES), lambda bi, hp: (bi, 0, k_group + hp)),
        ]

    in_specs = seq_specs(t) + seq_specs(ct) + [
        pl.BlockSpec((3, 2) + btab.shape[2:], lambda bi, hp: (0, hp, 0, 0))]
    out_specs = [pl.BlockSpec((None, t, LANES), lambda bi, hp: (bi, 0, hp))]
    out_shape = [jax.ShapeDtypeStruct((b, t, d_grp), BF16)]
    if need_ctx:
        out_specs.append(pl.BlockSpec((None, ct, LANES), lambda bi, hp: (bi, 0, hp)))
        out_shape.append(jax.ShapeDtypeStruct((b, ct, d_grp), BF16))
    kern = functools.partial(_na_kernel, need_ctx=need_ctx, rows=rows)
    res = pl.pallas_call(
        kern,
        grid=(b, n_pairs),
        in_specs=in_specs,
        out_specs=out_specs,
        out_shape=out_shape,
        scratch_shapes=[
            pltpu.VMEM((2, 2, kn, qn), F32),
            pltpu.VMEM((2, 2, kn, qn), BF16),
            pltpu.VMEM((2, 2, 8, qn), F32),
            pltpu.VMEM((2, 2, 8, qn), F32),
        ],
        compiler_params=pltpu.CompilerParams(
            dimension_semantics=("arbitrary", "arbitrary"), vmem_limit_bytes=VMEM_LIMIT),
        name="natten_ctx" if need_ctx else "natten",
    )(pfx, pfx, ptx, pfc, pfc, ptc, btab)
    return (res[0], res[1]) if need_ctx else (res[0], None)


def _out_mlp_kernel(x_ref, ym_ref, yn_ref, mod_ref, g_ref, wo_ref, w1_ref, w2_ref, fg_ref, o_ref,
                    *, d_model, final_norm):
    d = d_model
    dm = ym_ref.shape[1]
    att = _dot(ym_ref[...], wo_ref[0:dm, :]) + _dot(yn_ref[...], wo_ref[dm:, :])
    x1 = x_ref[...] + mod_ref[:, 2 * d:3 * d] * att
    h = _modulated_norm(x1, g_ref[...], mod_ref[:, 4 * d:5 * d], mod_ref[:, 3 * d:4 * d]).astype(BF16)
    d_ff = w1_ref.shape[1]
    acc = None
    for c in range(d_ff // COL_GROUP):
        hid = _dot(h, w1_ref[:, c * COL_GROUP:(c + 1) * COL_GROUP])
        hid = jnp.square(jnp.maximum(hid, 0.0)).astype(BF16)
        part = _dot(hid, w2_ref[c * COL_GROUP:(c + 1) * COL_GROUP, :])
        acc = part if acc is None else acc + part
    x2 = x1 + mod_ref[:, 5 * d:6 * d] * acc
    if final_norm:
        x2 = (x2 * lax.rsqrt(jnp.mean(x2 * x2, axis=-1, keepdims=True) + EPS)) * fg_ref[...]
    o_ref[...] = x2


def _out_mlp(x, ym, yn, mod4, mod_row, g, w_out, w1, w2, final_g, *, final_norm):
    b, t, d = x.shape
    dm = ym.shape[2]
    d_ff = w1.shape[1]
    tm = min(TOKEN_TILE, t)
    kern = functools.partial(_out_mlp_kernel, d_model=d, final_norm=final_norm)
    const = lambda i, bi: (0, 0)
    return pl.pallas_call(
        kern,
        grid=(t // tm, b),
        in_specs=[
            pl.BlockSpec((None, tm, d), lambda i, bi: (bi, i, 0)),
            pl.BlockSpec((None, tm, dm), lambda i, bi: (bi, i, 0)),
            pl.BlockSpec((None, tm, dm), lambda i, bi: (bi, i, 0)),
            pl.BlockSpec((None, 1, N_MOD * d), lambda i, bi: (mod_row(bi), 0, 0)),
            pl.BlockSpec((1, d), const),
            pl.BlockSpec((d, d), const, pipeline_mode=pl.Buffered(1)),
            pl.BlockSpec((d, d_ff), const, pipeline_mode=pl.Buffered(1)),
            pl.BlockSpec((d_ff, d), const, pipeline_mode=pl.Buffered(1)),
            pl.BlockSpec((1, d), const),
        ],
        out_specs=pl.BlockSpec((None, tm, d), lambda i, bi: (bi, i, 0)),
        out_shape=jax.ShapeDtypeStruct((b, t, d), F32),
        compiler_params=pltpu.CompilerParams(
            dimension_semantics=("arbitrary", "arbitrary"), vmem_limit_bytes=VMEM_LIMIT),
        name="out_mlp_final" if final_norm else "out_mlp",
    )(x, ym, yn, mod4, g, w_out, w1, w2, final_g)


def _rope_tables(n_tokens):
    t = jnp.arange(n_tokens)
    row = (t // GRID_W).astype(F32)
    col = (t % GRID_W).astype(F32)
    n_freq = HEAD_DIM // 4
    inv_freq = ROPE_BASE ** (-jnp.arange(n_freq, dtype=F32) / n_freq)
    ang = jnp.concatenate([row[:, None] * inv_freq, col[:, None] * inv_freq], axis=-1)
    cos, sin = jnp.cos(ang), jnp.sin(ang)
    reps = LANES // HEAD_DIM
    return (jnp.tile(jnp.concatenate([cos, cos], axis=-1), (1, reps)),
            jnp.tile(jnp.concatenate([-sin, sin], axis=-1), (1, reps)))


def kernel(x, c, ctx, c_ctx, w_ada, b_ada, norm1_g, w_in, b_gate, mlstm_norm_g, rpb, w_out, norm2_g,
           w_mlp1, w_mlp2, final_g):
    b, s, d = x.shape
    ct = ctx.shape[1]
    depth = w_ada.shape[0]
    d_grp = d // 2
    nh = d_grp // HEAD_DIM
    n_gate = b_gate.shape[1]
    n_main = w_in.shape[2] - n_gate
    assert n_main == 7 * d_grp and d_grp % LANES == 0 and n_gate == 4 * nh and n_gate <= LANES
    assert s % TOKEN_TILE == 0 and ct % MLSTM_L == 0 and (s // GRID_W) % NA_QROWS == 0

    n_rows = -(-(b + 1) // 8) * 8
    cc = jnp.zeros((n_rows, d), F32).at[:b].set(c).at[b].set(c_ctx)
    mod = _adaln(cc, w_ada, b_ada)
    mod = mod.reshape(depth, n_rows, 1, N_MOD * d)

    cos_t, sin_t = _rope_tables(s)
    grp = lambda k: w_in[:, :, k * d_grp:(k + 1) * d_grp]
    w_tok = jnp.concatenate([grp(1), grp(3), grp(5)], axis=2).astype(BF16)
    w_feat = jnp.swapaxes(
        jnp.concatenate([grp(0), grp(2), grp(4), grp(6), w_in[:, :, n_main:]], axis=2), 1, 2).astype(BF16)
    b_gate_p = jnp.broadcast_to(b_gate.astype(F32)[:, :, None], (depth, n_gate, LANES))
    w_out_b = w_out.astype(BF16)
    w1_b = w_mlp1.astype(BF16)
    w2_b = w_mlp2.astype(BF16)
    n_pairs = d_grp // LANES
    btab = _na_bias_table(rpb)

    x_row = lambda bi: bi
    c_row = lambda bi: b
    xc = ctx
    for l in range(depth):
        need_ctx = l < depth - 1
        g1 = norm1_g[l].reshape(1, d)
        g2 = norm2_g[l].reshape(1, d)
        ptx, pfx, gcx, grx = _in_proj(x, mod[l], x_row, g1, w_tok[l], w_feat[l], b_gate_p[l],
                                      cos_t, sin_t, rope=True, nh=nh)
        ptc, pfc, gcc, grc = _in_proj(xc, mod[l], c_row, g1, w_tok[l], w_feat[l], b_gate_p[l],
                                      cos_t[:ct], sin_t[:ct], rope=False, nh=nh)
        ym, ycm = _mlstm(ptx, pfx, gcx, grx, ptc, pfc, gcc, grc, mlstm_norm_g[l].reshape(1, d_grp),
                         need_ctx=need_ctx, d_grp=d_grp)
        yn, ycn = _na(ptx, pfx, ptc, pfc, btab[l], need_ctx=need_ctx, d_grp=d_grp,
                      k_group=2 * n_pairs, q_group=2 * n_pairs, v_group=3 * n_pairs)
        fg = final_g.reshape(1, d)
        x = _out_mlp(x, ym, yn, mod[l], x_row, g2, w_out_b[l], w1_b[l], w2_b[l], fg,
                     final_norm=(l == depth - 1))
        if need_ctx:
            xc = _out_mlp(xc, ycm, ycn, mod[l], c_row, g2, w_out_b[l], w1_b[l], w2_b[l], fg,
                          final_norm=False)
    return x
```

```python
import functools

import jax
import jax.numpy as jnp
import numpy as np
from jax import lax
from jax.experimental import pallas as pl
from jax.experimental.pallas import tpu as pltpu

HEAD_DIM = 64
GRID_W = 64
NA_ROWS = 8
NA_COLS = 16
N_MOD = 6
ROPE_BASE = 10000.0
EPS = 1e-6

LANES = 128
MLSTM_L = 128
NA_QROWS = 4
NA_WIN = NA_QROWS + NA_ROWS
NEG = -1e30
LOG2E = 1.4426950408889634
COL_GROUP = 512
TOKEN_TILE = 512
VMEM_LIMIT = 56 * 1024 * 1024

F32 = jnp.float32
BF16 = jnp.bfloat16


def _dot(a, b):
    return jnp.dot(a, b, preferred_element_type=F32)


def _dot_nt(a, b):
    return lax.dot_general(a, b, (((1,), (1,)), ((), ())), preferred_element_type=F32)


def _split3(a):
    p0 = a.astype(BF16)
    r1 = a - p0.astype(F32)
    p1 = r1.astype(BF16)
    p2 = (r1 - p1.astype(F32)).astype(BF16)
    return p0, p1, p2


def _sigmoid(v):
    return 1.0 / (1.0 + jnp.exp(-v))


def _log_sigmoid(v):
    return jnp.minimum(v, 0.0) - jnp.log(1.0 + jnp.exp(-jnp.abs(v)))


def _adaln_kernel(c_ref, w_ref, b_ref, o_ref):
    c = c_ref[...]
    s = (c * _sigmoid(c)).astype(BF16)
    o_ref[...] = _dot(s, w_ref[...].astype(BF16)) + b_ref[...]


def _adaln(cc, w_ada, b_ada):
    depth, d, n = w_ada.shape
    rows = cc.shape[0]
    tn = n // 4
    return pl.pallas_call(
        _adaln_kernel,
        grid=(depth, n // tn),
        in_specs=[
            pl.BlockSpec((rows, d), lambda l, j: (0, 0)),
            pl.BlockSpec((None, d, tn), lambda l, j: (l, 0, j)),
            pl.BlockSpec((None, 1, tn), lambda l, j: (l, 0, j)),
        ],
        out_specs=pl.BlockSpec((None, rows, tn), lambda l, j: (l, 0, j)),
        out_shape=jax.ShapeDtypeStruct((depth, rows, n), F32),
        compiler_params=pltpu.CompilerParams(
            dimension_semantics=("arbitrary", "arbitrary"), vmem_limit_bytes=VMEM_LIMIT),
        name="adaln",
    )(cc, w_ada, b_ada.reshape(depth, 1, n))


def _modulated_norm(x, g, scale, shift):
    y = x * lax.rsqrt(jnp.mean(x * x, axis=-1, keepdims=True) + EPS)
    return (y * g) * (1.0 + scale) + shift


def _in_proj_gates(g, gc_ref, gr_ref, nh):
    L = MLSTM_L
    lf = _log_sigmoid(g[2 * nh:4 * nh])
    u_ = lax.broadcasted_iota(jnp.int32, (L, L), 0)
    s_ = lax.broadcasted_iota(jnp.int32, (L, L), 1)
    tri_prefix = (u_ <= s_).astype(BF16)
    fwd_rows = lax.broadcasted_iota(jnp.int32, (2 * nh, L), 0) < nh
    pad = jnp.zeros((LANES - 8 * nh, L), F32)
    for c in range(g.shape[1] // L):
        lfc = lf[:, c * L:(c + 1) * L]
        b_pre = sum(_dot(p, tri_prefix) for p in _split3(lfc))
        tot = jnp.broadcast_to(jnp.sum(lfc, axis=1, keepdims=True), (2 * nh, L))
        b = jnp.where(fwd_rows, b_pre, tot - b_pre + lfc)
        a = g[0:2 * nh, c * L:(c + 1) * L] - b
        amax = jnp.broadcast_to(jnp.max(a, axis=1, keepdims=True), (2 * nh, L))
        out = jnp.concatenate([a, b, tot, amax], axis=0) * LOG2E
        gr_ref[c] = out
        gc_ref[c * L:(c + 1) * L, :] = jnp.concatenate([out, pad], axis=0).T


def _in_proj_kernel(x_ref, mod_ref, g_ref, wt_ref, wf_ref, bg_ref, cos_ref, sin_ref,
                    pt_ref, pf_ref, gc_ref, gr_ref, *, d_model, rope, nh):
    d = d_model
    tm = x_ref.shape[0]
    L = MLSTM_L
    n_chunks = tm // L
    h = _modulated_norm(x_ref[...], g_ref[...], mod_ref[:, d:2 * d], mod_ref[:, 0:d])
    hb = h.astype(BF16)
    q_scale = HEAD_DIM ** -0.5
    half = HEAD_DIM // 2

    lane = lax.broadcasted_iota(jnp.int32, (tm, LANES), 1)
    first_half = (lane % HEAD_DIM) < half
    for cg in range(wt_ref.shape[1] // COL_GROUP):
        acc = _dot(hb, wt_ref[:, cg * COL_GROUP:(cg + 1) * COL_GROUP])
        for s in range(COL_GROUP // LANES):
            slab = acc[:, s * LANES:(s + 1) * LANES]
            if rope and cg == 0:
                partner = jnp.where(first_half,
                                    pltpu.roll(slab, LANES - half, axis=1),
                                    pltpu.roll(slab, half, axis=1))
                slab = slab * cos_ref[...] + partner * sin_ref[...]
            pt_ref[cg * (COL_GROUP // LANES) + s] = slab.astype(BF16)

    d_grp = COL_GROUP
    part_scale = {0: q_scale, 2: q_scale * LOG2E}
    if rope:
        cos_t = cos_ref[...].T[0:half]
        sin_t = sin_ref[...].T[half:HEAD_DIM]
    n_parts = wf_ref.shape[0] // d_grp
    n_gate = wf_ref.shape[0] - n_parts * d_grp
    for part in range(n_parts):
        last = part == n_parts - 1
        acc = _dot_nt(wf_ref[part * d_grp:(part + 1) * d_grp + (n_gate if last else 0), :], hb)
        if last:
            _in_proj_gates(acc[d_grp:] + bg_ref[:, 0:1], gc_ref, gr_ref, nh)
        for hh in range(d_grp // HEAD_DIM):
            r0 = hh * HEAD_DIM
            if part == 0 and rope:
                x1 = acc[r0:r0 + half]
                x2 = acc[r0 + half:r0 + HEAD_DIM]
                blk = jnp.concatenate([x1 * cos_t - x2 * sin_t, x1 * sin_t + x2 * cos_t], axis=0)
            else:
                blk = acc[r0:r0 + HEAD_DIM]
            if part in part_scale:
                blk = blk * part_scale[part]
            blk = blk.astype(BF16)
            for c in range(n_chunks):
                pf_ref[c, part * d_grp + r0:part * d_grp + r0 + HEAD_DIM, :] = blk[:, c * L:(c + 1) * L]


def _in_proj(x, mod4, mod_row, g, w_tok, w_feat, b_gate, cos_t, sin_t, *, rope, nh):
    b, t, d = x.shape
    nt = w_tok.shape[1]
    nfw = w_feat.shape[0]
    nf = nfw - 4 * nh
    tm = min(TOKEN_TILE, t)
    nc = tm // MLSTM_L
    kern = functools.partial(_in_proj_kernel, d_model=d, rope=rope, nh=nh)
    const = lambda i, bi: (0, 0)
    return pl.pallas_call(
        kern,
        grid=(t // tm, b),
        in_specs=[
            pl.BlockSpec((None, tm, d), lambda i, bi: (bi, i, 0)),
            pl.BlockSpec((None, 1, N_MOD * d), lambda i, bi: (mod_row(bi), 0, 0)),
            pl.BlockSpec((1, d), const),
            pl.BlockSpec((d, nt), const, pipeline_mode=pl.Buffered(1)),
            pl.BlockSpec((nfw, d), const, pipeline_mode=pl.Buffered(1)),
            pl.BlockSpec((4 * nh, LANES), const),
            pl.BlockSpec((tm, LANES), lambda i, bi: (i, 0)),
            pl.BlockSpec((tm, LANES), lambda i, bi: (i, 0)),
        ],
        out_specs=[
            pl.BlockSpec((None, nt // LANES, tm, LANES), lambda i, bi: (bi, 0, i, 0)),
            pl.BlockSpec((None, nc, nf, MLSTM_L), lambda i, bi: (bi, i, 0, 0)),
            pl.BlockSpec((None, tm, LANES), lambda i, bi: (bi, i, 0)),
            pl.BlockSpec((None, nc, 8 * nh, MLSTM_L), lambda i, bi: (bi, i, 0, 0)),
        ],
        out_shape=[
            jax.ShapeDtypeStruct((b, nt // LANES, t, LANES), BF16),
            jax.ShapeDtypeStruct((b, t // MLSTM_L, nf, MLSTM_L), BF16),
            jax.ShapeDtypeStruct((b, t, LANES), F32),
            jax.ShapeDtypeStruct((b, t // MLSTM_L, 8 * nh, MLSTM_L), F32),
        ],
        compiler_params=pltpu.CompilerParams(
            dimension_semantics=("arbitrary", "arbitrary"), vmem_limit_bytes=VMEM_LIMIT),
        name="in_proj_rope" if rope else "in_proj",
    )(x, mod4, g, w_tok, w_feat, b_gate, cos_t, sin_t)


def _chunk_rows(j):
    if isinstance(j, int):
        return pl.ds(j * MLSTM_L, MLSTM_L)
    return pl.ds(pl.multiple_of(j * MLSTM_L, MLSTM_L), MLSTM_L)


N_REP = 16
AUG = HEAD_DIM + N_REP


def _mlstm_intra(qt_ref, vt_ref, k_ref, gc_ref, gr_ref, j, tau, hp, m_state, p_ref, rv_ref, u_ref,
                 *, rev, nh):
    L = MLSTM_L
    d = HEAD_DIM
    rows = _chunk_rows(j)
    gcb = gc_ref[rows, :]
    qt = qt_ref[j]
    k2 = k_ref[rows, :]
    lane = lax.broadcasted_iota(jnp.int32, (L, LANES), 1)
    sub = lax.broadcasted_iota(jnp.int32, (L, LANES), 0)
    tri = (sub >= lane) if rev else (sub <= lane)
    zero = jnp.zeros_like(qt)
    qbd = jnp.concatenate([jnp.where(sub < d, qt, zero), jnp.where(sub >= d, qt, zero)], axis=1)
    st = _dot(k2, qbd)
    base = nh if rev else 0
    m_new, ws = [], []
    for hd in range(2):
        r = base + 2 * hp + hd
        a_col = jnp.sum(jnp.where(lane == r, gcb, 0.0), axis=1, keepdims=True)
        a_row = gr_ref[j, pl.ds(r, 1), :]
        b_row = gr_ref[j, pl.ds(2 * nh + r, 1), :]
        tot = gr_ref[j, pl.ds(4 * nh + r, 1), :]
        amax = gr_ref[j, pl.ds(6 * nh + r, 1), :]
        m_h = m_state[hd]
        at = jnp.where(tri, a_col, NEG)
        mrow = jnp.maximum(jnp.max(at, axis=0, keepdims=True), m_h)
        p_ref[int(rev), tau, :, hd * L:(hd + 1) * L] = (
            st[:, hd * L:(hd + 1) * L] * jnp.exp2(at - mrow)).astype(BF16)
        m_end = jnp.maximum(m_h, amax)
        ws.append(jnp.exp2(a_row - m_end))
        rv_ref[int(rev), tau, 4 * hd:4 * hd + 4, :] = jnp.concatenate([
            jnp.exp2(m_h - mrow),
            jnp.exp2(-(b_row + mrow)),
            ws[hd],
            jnp.exp2(m_h - m_end),
        ], axis=0)
        m_new.append(tot + m_end)
    vf = vt_ref[j].astype(F32)
    vsc = jnp.concatenate(
        [(vf[hd * d:(hd + 1) * d] * ws[hd]).astype(BF16) for hd in range(2)]
        + [jnp.broadcast_to(ws[hd], (N_REP, L)).astype(BF16) for hd in range(2)], axis=0)
    u_ref[int(rev), tau] = jnp.where(_own_features(), _dot(vsc, k2), 0.0)
    return m_new


def _head_a_rows():
    row = lax.broadcasted_iota(jnp.int32, (2 * AUG, LANES), 0)
    return (row < HEAD_DIM) | ((row >= 2 * HEAD_DIM) & (row < 2 * HEAD_DIM + N_REP))


def _own_features():
    lane = lax.broadcasted_iota(jnp.int32, (2 * AUG, LANES), 1)
    return _head_a_rows() == (lane < HEAD_DIM)


def _mlstm_chunk(qt_ref, vt_ref, j, tau, ct_ref, p_ref, rv_ref, u_ref, hacc_ref, *, rev):
    L = MLSTM_L
    d = HEAD_DIM
    qt = qt_ref[j]
    vt = vt_ref[j]
    rv = rv_ref[int(rev), tau]
    wprev, eneg, _, wc = ([rv[4 * hd + i:4 * hd + i + 1] for hd in range(2)] for i in range(4))
    qf = qt.astype(F32)
    qw = jnp.concatenate([(qf * wprev[0]).astype(BF16), (qf * wprev[1]).astype(BF16)], axis=1)
    rhs = jnp.concatenate([p_ref[int(rev), tau], qw], axis=0)
    ones = jnp.ones((2 * N_REP, L), BF16)
    lhs = jnp.concatenate([jnp.concatenate([vt, ones], axis=0), ct_ref[...].astype(BF16)], axis=1)
    res = _dot(lhs, rhs)
    hs = []
    for hd in range(2):
        num = res[hd * d:(hd + 1) * d, hd * L:(hd + 1) * L]
        den = res[2 * d + hd * N_REP:2 * d + hd * N_REP + 1, hd * L:(hd + 1) * L]
        hs.append(num * (1.0 / jnp.maximum(jnp.abs(den), eneg[hd])))
    hacc_ref[int(rev), j] = jnp.concatenate(hs, axis=0)
    ct_ref[...] = jnp.where(_head_a_rows(), wc[0], wc[1]) * ct_ref[...] + u_ref[int(rev), tau]


def _mlstm_finish(hacc_ref, o_ref, g_ref, y_ref, n_chunks):
    sub = lax.broadcasted_iota(jnp.int32, (LANES, MLSTM_L), 0)
    row_a = sub < HEAD_DIM

    def head_mean(v):
        ma = jnp.mean(v[:HEAD_DIM], axis=0, keepdims=True)
        mb = jnp.mean(v[HEAD_DIM:], axis=0, keepdims=True)
        return jnp.where(row_a, ma, mb)

    def body(j, carry):
        rows = _chunk_rows(j)
        hs = hacc_ref[0, j] + hacc_ref[1, j]
        dv = hs - head_mean(hs)
        yn = (dv * lax.rsqrt(head_mean(dv * dv) + EPS)).T
        gate = _sigmoid(o_ref[rows, :].astype(F32))
        y_ref[rows, :] = (yn * g_ref[...] * gate).astype(y_ref.dtype)
        return carry

    lax.fori_loop(0, n_chunks, body, 0, unroll=min(4, n_chunks))


def _mlstm_kernel(qtx_ref, vtx_ref, kx_ref, ox_ref, gcx_ref, grx_ref,
                  qtc_ref, vtc_ref, kc_ref, oc_ref, gcc_ref, grc_ref, g_ref, *rest, need_ctx, nh):
    if need_ctx:
        y_ref, yc_ref, cf_ref, cb_ref, p_ref, rv_ref, u_ref, hx_ref, hc_ref = rest
    else:
        y_ref, cf_ref, cb_ref, p_ref, rv_ref, u_ref, hx_ref, hc_ref = rest
        yc_ref = None
    hp = pl.program_id(1)
    ncx = qtx_ref.shape[0]
    ncc = qtc_ref.shape[0]
    intra = functools.partial(_mlstm_intra, hp=hp, p_ref=p_ref, rv_ref=rv_ref, u_ref=u_ref, nh=nh)
    zero = jnp.zeros((1, MLSTM_L), F32)
    mf = [zero, zero]
    mb = [zero, zero]
    for j in range(ncc):
        mf = intra(qtc_ref, vtc_ref, kc_ref, gcc_ref, grc_ref, j, j, m_state=mf, rev=False)
        mb = intra(qtc_ref, vtc_ref, kc_ref, gcc_ref, grc_ref, ncc - 1 - j, j, m_state=mb, rev=True)

    def intra_body(j, carry):
        mf_ = intra(qtx_ref, vtx_ref, kx_ref, gcx_ref, grx_ref, j, ncc + j,
                    m_state=list(carry[0:2]), rev=False)
        mb_ = intra(qtx_ref, vtx_ref, kx_ref, gcx_ref, grx_ref, ncx - 1 - j, ncc + j,
                    m_state=list(carry[2:4]), rev=True)
        return (mf_[0], mf_[1], mb_[0], mb_[1])

    lax.fori_loop(0, ncx, intra_body, (mf[0], mf[1], mb[0], mb[1]), unroll=2)

    cf_ref[...] = jnp.zeros_like(cf_ref)
    cb_ref[...] = jnp.zeros_like(cb_ref)
    step = functools.partial(_mlstm_chunk, p_ref=p_ref, rv_ref=rv_ref, u_ref=u_ref)
    for j in range(ncc):
        step(qtc_ref, vtc_ref, j, j, cf_ref, hacc_ref=hc_ref, rev=False)
        step(qtc_ref, vtc_ref, ncc - 1 - j, j, cb_ref, hacc_ref=hc_ref, rev=True)

    def body(j, carry):
        step(qtx_ref, vtx_ref, j, ncc + j, cf_ref, hacc_ref=hx_ref, rev=False)
        step(qtx_ref, vtx_ref, ncx - 1 - j, ncc + j, cb_ref, hacc_ref=hx_ref, rev=True)
        return carry

    lax.fori_loop(0, ncx, body, 0, unroll=4)
    _mlstm_finish(hx_ref, ox_ref, g_ref, y_ref, ncx)
    if need_ctx:
        _mlstm_finish(hc_ref, oc_ref, g_ref, yc_ref, ncc)


def _mlstm(ptx, pfx, gcx, grx, ptc, pfc, gcc, grc, norm_g, *, need_ctx, d_grp):
    b, _, t, _ = ptx.shape
    ct = ptc.shape[2]
    nh = d_grp // HEAD_DIM
    n_pairs = d_grp // LANES
    n_steps = (t + ct) // MLSTM_L

    def seq_specs(tlen, pf, gr):
        nc = tlen // MLSTM_L
        return [
            pl.BlockSpec((None, nc, LANES, MLSTM_L), lambda bi, hp: (bi, 0, hp, 0)),
            pl.BlockSpec((None, nc, LANES, MLSTM_L), lambda bi, hp: (bi, 0, n_pairs + hp, 0)),
            pl.BlockSpec((None, None, tlen, LANES), lambda bi, hp: (bi, hp, 0, 0)),
            pl.BlockSpec((None, None, tlen, LANES), lambda bi, hp: (bi, n_pairs + hp, 0, 0)),
            pl.BlockSpec((None, tlen, LANES), lambda bi, hp: (bi, 0, 0)),
            pl.BlockSpec((None,) + gr.shape[1:], lambda bi, hp: (bi, 0, 0, 0)),
        ]

    out_specs = [pl.BlockSpec((None, None, t, LANES), lambda bi, hp: (bi, hp, 0, 0))]
    out_shape = [jax.ShapeDtypeStruct((b, n_pairs, t, LANES), BF16)]
    if need_ctx:
        out_specs.append(pl.BlockSpec((None, None, ct, LANES), lambda bi, hp: (bi, hp, 0, 0)))
        out_shape.append(jax.ShapeDtypeStruct((b, n_pairs, ct, LANES), BF16))
    kern = functools.partial(_mlstm_kernel, need_ctx=need_ctx, nh=nh)
    res = pl.pallas_call(
        kern,
        grid=(b, n_pairs),
        in_specs=seq_specs(t, pfx, grx) + seq_specs(ct, pfc, grc) + [
            pl.BlockSpec((1, LANES), lambda bi, hp: (0, hp))],
        out_specs=out_specs,
        out_shape=out_shape,
        scratch_shapes=[
            pltpu.VMEM((2 * AUG, LANES), F32),
            pltpu.VMEM((2 * AUG, LANES), F32),
            pltpu.VMEM((2, n_steps, MLSTM_L, 2 * MLSTM_L), BF16),
            pltpu.VMEM((2, n_steps, 8, MLSTM_L), F32),
            pltpu.VMEM((2, n_steps, 2 * AUG, LANES), F32),
            pltpu.VMEM((2, t // MLSTM_L, LANES, MLSTM_L), F32),
            pltpu.VMEM((2, ct // MLSTM_L, LANES, MLSTM_L), F32),
        ],
        compiler_params=pltpu.CompilerParams(
            dimension_semantics=("arbitrary", "arbitrary"), vmem_limit_bytes=VMEM_LIMIT),
        name="mlstm_ctx" if need_ctx else "mlstm",
    )(pfx, pfx, ptx, ptx, gcx, grx, pfc, pfc, ptc, ptc, gcc, grc, norm_g)
    return (res[0], res[1]) if need_ctx else (res[0], None)


NA_KEY_GROUP = 256


def _na_softmax(s_ref, p_ref, idx, n_keys, n_q, col_max):
    sums = []
    for lt in range(n_q // LANES):
        cols = slice(lt * LANES, (lt + 1) * LANES)
        m = col_max[:, cols]
        tot = None
        for kg in range(n_keys // NA_KEY_GROUP):
            r = slice(kg * NA_KEY_GROUP, (kg + 1) * NA_KEY_GROUP)
            p = jnp.exp2(s_ref[idx + (r, cols)] - m)
            p_ref[idx + (r, cols)] = p.astype(BF16)
            sk = jnp.sum(p, axis=0, keepdims=True)
            tot = sk if tot is None else tot + sk
        sums.append(tot)
    return jnp.concatenate(sums, axis=1)


def _na_kernel(qt_ref, vt_ref, k_ref, qtc_ref, vtc_ref, kc_ref, bt_ref, *rest, need_ctx, rows):
    if need_ctx:
        y_ref, yc_ref, s_ref, p_ref, m_ref, l_ref = rest
    else:
        y_ref, s_ref, p_ref, m_ref, l_ref = rest
        yc_ref = None
    w = GRID_W
    L = MLSTM_L
    qn = NA_QROWS * w
    kn = NA_WIN * w
    n_ctx = kc_ref.shape[0]
    nblk = rows // NA_QROWS
    kc = kc_ref[...]
    vtc = jnp.concatenate([vtc_ref[c] for c in range(n_ctx // L)], axis=1)
    sub = lax.broadcasted_iota(jnp.int32, (LANES, qn), 0)
    row_a = sub < HEAD_DIM

    def first_key_row(ib):
        if isinstance(ib, int):
            return min(max(ib * NA_QROWS - NA_ROWS // 2, 0), rows - NA_WIN)
        return jnp.clip(ib * NA_QROWS - NA_ROWS // 2, 0, rows - NA_WIN)

    def token_rows(start, n):
        if isinstance(start, int):
            return pl.ds(start, n)
        return pl.ds(pl.multiple_of(start, w), n)

    def chunks(ref, tok0, n_tok):
        c0 = tok0 // L
        return jnp.concatenate([ref[c0 + c] for c in range(n_tok // L)], axis=1)

    def head_queries(qt, hd):
        return jnp.where(row_a if hd == 0 else jnp.logical_not(row_a), qt, jnp.zeros_like(qt))

    def scores(ib, buf):
        if isinstance(ib, int):
            var = 0 if ib == 0 else (2 if ib == nblk - 1 else 1)
        else:
            var = jnp.where(ib == nblk - 1, 2, 1)
        qt = chunks(qt_ref, ib * qn, qn)
        kw = k_ref[token_rows(first_key_row(ib) * w, kn), :]
        for hd in range(2):
            qh = head_queries(qt, hd)
            s_loc = _dot(kw, qh) + bt_ref[var, hd]
            s_ctx = _dot(kc, qh)
            s_ref[buf, hd, 0:kn, :] = s_loc
            s_ref[buf, hd, kn:, :] = s_ctx
            m_ref[buf, hd] = jnp.broadcast_to(
                jnp.maximum(jnp.max(s_loc, axis=0, keepdims=True), jnp.max(s_ctx, axis=0, keepdims=True)),
                (8, qn))

    def softmax(buf):
        for hd in range(2):
            l_ref[buf, hd] = jnp.broadcast_to(
                _na_softmax(s_ref, p_ref, (buf, hd), kn + n_ctx, qn, m_ref[buf, hd, 0:1, :]), (8, qn))

    def values(ib, buf):
        vt = jnp.concatenate([chunks(vt_ref, first_key_row(ib) * w, kn), vtc], axis=1)
        outs = [_dot(vt, p_ref[buf, hd]) * (1.0 / l_ref[buf, hd, 0:1, :]) for hd in range(2)]
        y_ref[token_rows(ib * qn, qn), :] = jnp.where(row_a, outs[0], outs[1]).T.astype(y_ref.dtype)

    scores(0, 0)
    scores(1, 1)
    softmax(0)

    def body(tt, carry):
        t = 2 * tt + 1
        scores(t + 1, 0)
        softmax(1)
        values(t - 1, 0)
        scores(t + 2, 1)
        softmax(0)
        values(t, 1)
        return carry

    lax.fori_loop(0, (nblk - 2) // 2, body, 0)
    softmax(1)
    values(nblk - 2, 0)
    values(nblk - 1, 1)

    if need_ctx:
        qtc = jnp.concatenate([qtc_ref[c] for c in range(n_ctx // L)], axis=1)
        sub_c = lax.broadcasted_iota(jnp.int32, qtc.shape, 0)
        row_ac = sub_c < HEAD_DIM
        for c0 in range(0, n_ctx, qn):
            outs = []
            for hd in range(2):
                qh = jnp.where(row_ac if hd == 0 else jnp.logical_not(row_ac), qtc, jnp.zeros_like(qtc))
                s_cc = _dot(kc, qh[:, c0:c0 + qn])
                s_ref[0, hd, 0:n_ctx, :] = s_cc
                lsum = _na_softmax(s_ref, p_ref, (0, hd), n_ctx, qn, jnp.max(s_cc, axis=0, keepdims=True))
                outs.append(_dot(vtc, p_ref[0, hd, 0:n_ctx, :]) * (1.0 / lsum))
            yc_ref[c0:c0 + qn, :] = jnp.where(row_a, outs[0], outs[1]).T.astype(yc_ref.dtype)


def _na_bias_kernel(r_ref, o_ref):
    w = GRID_W
    lane = lax.broadcasted_iota(jnp.int32, (w, LANES), 1)
    k = lax.broadcasted_iota(jnp.int32, (w, LANES), 0)
    col_start = jnp.clip(lane % w - NA_COLS // 2, 0, w - NA_COLS)
    col_ok = (k >= col_start) & (k < col_start + NA_COLS)
    left = lane < w
    lane_row = lax.broadcasted_iota(jnp.int32, (1, LANES), 1)
    neg = jnp.full((w, LANES), NEG, F32)
    n_dr = 2 * NA_ROWS - 1
    cache = {}

    def pair_tile(dr_a, ok_a, ok_b):
        key = (dr_a, ok_a, ok_b)
        if key not in cache:
            if not (ok_a or ok_b):
                cache[key] = neg
            else:
                ia = min(max(dr_a, 0), n_dr - 1)
                ib = min(max(dr_a - 1, 0), n_dr - 1)
                src = jnp.where(lane_row < w, r_ref[ia:ia + 1, :], pltpu.roll(r_ref[ib:ib + 1, :], w, axis=1))
                toep = pltpu.roll(jnp.broadcast_to(src, (w, LANES)), LANES - (NA_COLS - 1), axis=1,
                                  stride=1, stride_axis=0)
                ok = col_ok
                if not ok_a:
                    ok = ok & jnp.logical_not(left)
                if not ok_b:
                    ok = ok & left
                cache[key] = jnp.where(ok, toep * LOG2E, neg)
        return cache[key]

    offs = (0, NA_ROWS // 2, NA_ROWS)
    for v in range(3):
        for j in range(NA_WIN):
            for ii in range(NA_QROWS // 2):
                i = 2 * ii
                lo_a, lo_b = ((0, 0), (i, i + 1), (NA_QROWS, NA_QROWS))[v]
                dr_a = j - i + (NA_ROWS - 1) - offs[v]
                ok_a = lo_a <= j < lo_a + NA_ROWS
                ok_b = lo_b <= j < lo_b + NA_ROWS
                o_ref[v, j * w:(j + 1) * w, ii * LANES:(ii + 1) * LANES] = pair_tile(dr_a, ok_a, ok_b)


def _na_bias_table(rpb):
    depth, h, nr, nc = rpb.shape
    assert nr == 2 * NA_ROWS - 1 and nc == 2 * NA_COLS - 1 and 2 * GRID_W == LANES
    r_pad = jnp.pad(rpb.astype(F32)[..., ::-1], ((0, 0), (0, 0), (0, 16 - nr), (0, LANES - nc)))
    qn, kn = NA_QROWS * GRID_W, NA_WIN * GRID_W
    return pl.pallas_call(
        _na_bias_kernel,
        grid=(depth, h),
        in_specs=[pl.BlockSpec((None, None, 16, LANES), lambda l, hh: (l, hh, 0, 0))],
        out_specs=pl.BlockSpec((None, 3, None, kn, qn), lambda l, hh: (l, 0, hh, 0, 0)),
        out_shape=jax.ShapeDtypeStruct((depth, 3, h, kn, qn), F32),
        compiler_params=pltpu.CompilerParams(
            dimension_semantics=("arbitrary", "arbitrary"), vmem_limit_bytes=VMEM_LIMIT),
        name="natten_bias",
    )(r_pad)


def _na(ptx, pfx, ptc, pfc, btab, *, need_ctx, d_grp, k_group, q_group, v_group):
    b, _, t, _ = ptx.shape
    ct = ptc.shape[2]
    n_pairs = d_grp // LANES
    rows = t // GRID_W
    qn, kn = NA_QROWS * GRID_W, NA_WIN * GRID_W + ct
    assert (rows // NA_QROWS) % 2 == 0 and rows // NA_QROWS >= 4
    assert ct % qn == 0 and ct % NA_KEY_GROUP == 0 and kn % NA_KEY_GROUP == 0 and qn % MLSTM_L == 0

    def seq_specs(tlen):
        nc = tlen // MLSTM_L
        return [
            pl.BlockSpec((None, nc, LANES, MLSTM_L), lambda bi, hp: (bi, 0, q_group + hp, 0)),
            pl.BlockSpec((None, nc, LANES, MLSTM_L), lambda bi, hp: (bi, 0, v_group + hp, 0)),
            pl.BlockSpec((None, None, tlen, LANES), lambda bi, hp: (bi, k_group + hp, 0, 0)),
        ]

    in_specs = seq_specs(t) + seq_specs(ct) + [
        pl.BlockSpec((3, 2) + btab.shape[2:], lambda bi, hp: (0, hp, 0, 0))]
    out_specs = [pl.BlockSpec((None, None, t, LANES), lambda bi, hp: (bi, hp, 0, 0))]
    out_shape = [jax.ShapeDtypeStruct((b, n_pairs, t, LANES), BF16)]
    if need_ctx:
        out_specs.append(pl.BlockSpec((None, None, ct, LANES), lambda bi, hp: (bi, hp, 0, 0)))
        out_shape.append(jax.ShapeDtypeStruct((b, n_pairs, ct, LANES), BF16))
    kern = functools.partial(_na_kernel, need_ctx=need_ctx, rows=rows)
    res = pl.pallas_call(
        kern,
        grid=(b, n_pairs),
        in_specs=in_specs,
        out_specs=out_specs,
        out_shape=out_shape,
        scratch_shapes=[
            pltpu.VMEM((2, 2, kn, qn), F32),
            pltpu.VMEM((2, 2, kn, qn), BF16),
            pltpu.VMEM((2, 2, 8, qn), F32),
            pltpu.VMEM((2, 2, 8, qn), F32),
        ],
        compiler_params=pltpu.CompilerParams(
            dimension_semantics=("arbitrary", "arbitrary"), vmem_limit_bytes=VMEM_LIMIT),
        name="natten_ctx" if need_ctx else "natten",
    )(pfx, pfx, ptx, pfc, pfc, ptc, btab)
    return (res[0], res[1]) if need_ctx else (res[0], None)


def _out_mlp_kernel(x_ref, ym_ref, yn_ref, mod_ref, g_ref, wo_ref, w1_ref, w2_ref, fg_ref, o_ref,
                    *, d_model, final_norm):
    d = d_model
    y = jnp.concatenate([r[g] for r in (ym_ref, yn_ref) for g in range(r.shape[0])], axis=1)
    att = _dot(y, wo_ref[...])
    x1 = x_ref[...] + mod_ref[:, 2 * d:3 * d] * att
    h = _modulated_norm(x1, g_ref[...], mod_ref[:, 4 * d:5 * d], mod_ref[:, 3 * d:4 * d]).astype(BF16)
    d_ff = w1_ref.shape[1]
    acc = None
    for c in range(d_ff // COL_GROUP):
        hid = _dot(h, w1_ref[:, c * COL_GROUP:(c + 1) * COL_GROUP])
        hid = jnp.square(jnp.maximum(hid, 0.0)).astype(BF16)
        part = _dot(hid, w2_ref[c * COL_GROUP:(c + 1) * COL_GROUP, :])
        acc = part if acc is None else acc + part
    x2 = x1 + mod_ref[:, 5 * d:6 * d] * acc
    if final_norm:
        x2 = (x2 * lax.rsqrt(jnp.mean(x2 * x2, axis=-1, keepdims=True) + EPS)) * fg_ref[...]
    o_ref[...] = x2


def _out_mlp(x, ym, yn, mod4, mod_row, g, w_out, w1, w2, final_g, *, final_norm):
    b, t, d = x.shape
    n_pairs = ym.shape[1]
    d_ff = w1.shape[1]
    tm = min(TOKEN_TILE, t)
    kern = functools.partial(_out_mlp_kernel, d_model=d, final_norm=final_norm)
    const = lambda i, bi: (0, 0)
    return pl.pallas_call(
        kern,
        grid=(t // tm, b),
        in_specs=[
            pl.BlockSpec((None, tm, d), lambda i, bi: (bi, i, 0)),
            pl.BlockSpec((None, n_pairs, tm, LANES), lambda i, bi: (bi, 0, i, 0)),
            pl.BlockSpec((None, n_pairs, tm, LANES), lambda i, bi: (bi, 0, i, 0)),
            pl.BlockSpec((None, 1, N_MOD * d), lambda i, bi: (mod_row(bi), 0, 0)),
            pl.BlockSpec((1, d), const),
            pl.BlockSpec((d, d), const, pipeline_mode=pl.Buffered(1)),
            pl.BlockSpec((d, d_ff), const, pipeline_mode=pl.Buffered(1)),
            pl.BlockSpec((d_ff, d), const, pipeline_mode=pl.Buffered(1)),
            pl.BlockSpec((1, d), const),
        ],
        out_specs=pl.BlockSpec((None, tm, d), lambda i, bi: (bi, i, 0)),
        out_shape=jax.ShapeDtypeStruct((b, t, d), F32),
        compiler_params=pltpu.CompilerParams(
            dimension_semantics=("arbitrary", "arbitrary"), vmem_limit_bytes=VMEM_LIMIT),
        name="out_mlp_final" if final_norm else "out_mlp",
    )(x, ym, yn, mod4, g, w_out, w1, w2, final_g)


def _rope_tables(n_tokens):
    t = jnp.arange(n_tokens)
    row = (t // GRID_W).astype(F32)
    col = (t % GRID_W).astype(F32)
    n_freq = HEAD_DIM // 4
    inv_freq = ROPE_BASE ** (-jnp.arange(n_freq, dtype=F32) / n_freq)
    ang = jnp.concatenate([row[:, None] * inv_freq, col[:, None] * inv_freq], axis=-1)
    cos, sin = jnp.cos(ang), jnp.sin(ang)
    reps = LANES // HEAD_DIM
    return (jnp.tile(jnp.concatenate([cos, cos], axis=-1), (1, reps)),
            jnp.tile(jnp.concatenate([-sin, sin], axis=-1), (1, reps)))


def kernel(x, c, ctx, c_ctx, w_ada, b_ada, norm1_g, w_in, b_gate, mlstm_norm_g, rpb, w_out, norm2_g,
           w_mlp1, w_mlp2, final_g):
    b, s, d = x.shape
    ct = ctx.shape[1]
    depth = w_ada.shape[0]
    d_grp = d // 2
    nh = d_grp // HEAD_DIM
    n_gate = b_gate.shape[1]
    n_main = w_in.shape[2] - n_gate
    assert n_main == 7 * d_grp and d_grp % LANES == 0 and n_gate == 4 * nh and n_gate <= LANES
    assert s % TOKEN_TILE == 0 and ct % MLSTM_L == 0 and (s // GRID_W) % NA_QROWS == 0

    n_rows = -(-(b + 1) // 8) * 8
    cc = jnp.zeros((n_rows, d), F32).at[:b].set(c).at[b].set(c_ctx)
    mod = _adaln(cc, w_ada, b_ada)
    mod = mod.reshape(depth, n_rows, 1, N_MOD * d)

    cos_t, sin_t = _rope_tables(s)
    grp = lambda k: w_in[:, :, k * d_grp:(k + 1) * d_grp]
    w_tok = jnp.concatenate([grp(1), grp(3), grp(5)], axis=2).astype(BF16)
    w_feat = jnp.swapaxes(
        jnp.concatenate([grp(0), grp(2), grp(4), grp(6), w_in[:, :, n_main:]], axis=2), 1, 2).astype(BF16)
    b_gate_p = jnp.broadcast_to(b_gate.astype(F32)[:, :, None], (depth, n_gate, LANES))
    w_out_b = w_out.astype(BF16)
    w1_b = w_mlp1.astype(BF16)
    w2_b = w_mlp2.astype(BF16)
    n_pairs = d_grp // LANES
    btab = _na_bias_table(rpb)

    x_row = lambda bi: bi
    c_row = lambda bi: b
    xc = ctx
    for l in range(depth):
        need_ctx = l < depth - 1
        g1 = norm1_g[l].reshape(1, d)
        g2 = norm2_g[l].reshape(1, d)
        ptx, pfx, gcx, grx = _in_proj(x, mod[l], x_row, g1, w_tok[l], w_feat[l], b_gate_p[l],
                                      cos_t, sin_t, rope=True, nh=nh)
        ptc, pfc, gcc, grc = _in_proj(xc, mod[l], c_row, g1, w_tok[l], w_feat[l], b_gate_p[l],
                                      cos_t[:ct], sin_t[:ct], rope=False, nh=nh)
        ym, ycm = _mlstm(ptx, pfx, gcx, grx, ptc, pfc, gcc, grc, mlstm_norm_g[l].reshape(1, d_grp),
                         need_ctx=need_ctx, d_grp=d_grp)
        yn, ycn = _na(ptx, pfx, ptc, pfc, btab[l], need_ctx=need_ctx, d_grp=d_grp,
                      k_group=2 * n_pairs, q_group=2 * n_pairs, v_group=3 * n_pairs)
        fg = final_g.reshape(1, d)
        x = _out_mlp(x, ym, yn, mod[l], x_row, g2, w_out_b[l], w1_b[l], w2_b[l], fg,
                     final_norm=(l == depth - 1))
        if need_ctx:
            xc = _out_mlp(xc, ycm, ycn, mod[l], c_row, g2, w_out_b[l], w1_b[l], w2_b[l], fg,
                          final_norm=False)
    return x
```

```python
import functools

import jax
import jax.numpy as jnp
import numpy as np
from jax import lax
from jax.experimental import pallas as pl
from jax.experimental.pallas import tpu as pltpu

HEAD_DIM = 64
GRID_W = 64
NA_ROWS = 8
NA_COLS = 16
N_MOD = 6
ROPE_BASE = 10000.0
EPS = 1e-6

LANES = 128
MLSTM_L = 128
NA_QROWS = 4
NA_WIN = NA_QROWS + NA_ROWS
NEG = -1e30
LOG2E = 1.4426950408889634
COL_GROUP = 512
TOKEN_TILE = 512
VMEM_LIMIT = 56 * 1024 * 1024

F32 = jnp.float32
BF16 = jnp.bfloat16


def _dot(a, b):
    return jnp.dot(a, b, preferred_element_type=F32)


def _dot_nt(a, b):
    return lax.dot_general(a, b, (((1,), (1,)), ((), ())), preferred_element_type=F32)


def _split3(a):
    p0 = a.astype(BF16)
    r1 = a - p0.astype(F32)
    p1 = r1.astype(BF16)
    p2 = (r1 - p1.astype(F32)).astype(BF16)
    return p0, p1, p2


def _sigmoid(v):
    return 1.0 / (1.0 + jnp.exp(-v))


def _log_sigmoid(v):
    return jnp.minimum(v, 0.0) - jnp.log(1.0 + jnp.exp(-jnp.abs(v)))


def _adaln_kernel(c_ref, w_ref, b_ref, o_ref):
    c = c_ref[...]
    s = (c * _sigmoid(c)).astype(BF16)
    o_ref[...] = _dot(s, w_ref[...].astype(BF16)) + b_ref[...]


def _adaln(cc, w_ada, b_ada):
    depth, d, n = w_ada.shape
    rows = cc.shape[0]
    tn = n // 4
    return pl.pallas_call(
        _adaln_kernel,
        grid=(depth, n // tn),
        in_specs=[
            pl.BlockSpec((rows, d), lambda l, j: (0, 0)),
            pl.BlockSpec((None, d, tn), lambda l, j: (l, 0, j)),
            pl.BlockSpec((None, 1, tn), lambda l, j: (l, 0, j)),
        ],
        out_specs=pl.BlockSpec((None, rows, tn), lambda l, j: (l, 0, j)),
        out_shape=jax.ShapeDtypeStruct((depth, rows, n), F32),
        compiler_params=pltpu.CompilerParams(
            dimension_semantics=("arbitrary", "arbitrary"), vmem_limit_bytes=VMEM_LIMIT),
        name="adaln",
    )(cc, w_ada, b_ada.reshape(depth, 1, n))


def _modulated_norm(x, g, scale, shift):
    y = x * lax.rsqrt(jnp.mean(x * x, axis=-1, keepdims=True) + EPS)
    return (y * g) * (1.0 + scale) + shift


def _in_proj_gates(g, gc_ref, gr_ref, nh):
    L = MLSTM_L
    lf = _log_sigmoid(g[2 * nh:4 * nh])
    u_ = lax.broadcasted_iota(jnp.int32, (L, L), 0)
    s_ = lax.broadcasted_iota(jnp.int32, (L, L), 1)
    tri_prefix = (u_ <= s_).astype(BF16)
    fwd_rows = lax.broadcasted_iota(jnp.int32, (2 * nh, L), 0) < nh
    pad = jnp.zeros((LANES - 8 * nh, L), F32)
    for c in range(g.shape[1] // L):
        lfc = lf[:, c * L:(c + 1) * L]
        b_pre = sum(_dot(p, tri_prefix) for p in _split3(lfc))
        tot = jnp.broadcast_to(jnp.sum(lfc, axis=1, keepdims=True), (2 * nh, L))
        b = jnp.where(fwd_rows, b_pre, tot - b_pre + lfc)
        a = g[0:2 * nh, c * L:(c + 1) * L] - b
        amax = jnp.broadcast_to(jnp.max(a, axis=1, keepdims=True), (2 * nh, L))
        out = jnp.concatenate([a, b, tot, amax], axis=0) * LOG2E
        gr_ref[c] = out
        gc_ref[c * L:(c + 1) * L, :] = jnp.concatenate([out, pad], axis=0).T


def _in_proj_kernel(x_ref, mod_ref, g_ref, wt_ref, wf_ref, bg_ref, cos_ref, sin_ref,
                    pt_ref, pf_ref, gc_ref, gr_ref, *, d_model, rope, nh):
    d = d_model
    tm = x_ref.shape[0]
    L = MLSTM_L
    n_chunks = tm // L
    h = _modulated_norm(x_ref[...], g_ref[...], mod_ref[:, d:2 * d], mod_ref[:, 0:d])
    hb = h.astype(BF16)
    q_scale = HEAD_DIM ** -0.5
    half = HEAD_DIM // 2

    lane = lax.broadcasted_iota(jnp.int32, (tm, LANES), 1)
    first_half = (lane % HEAD_DIM) < half
    for cg in range(wt_ref.shape[1] // COL_GROUP):
        acc = _dot(hb, wt_ref[:, cg * COL_GROUP:(cg + 1) * COL_GROUP])
        for s in range(COL_GROUP // LANES):
            slab = acc[:, s * LANES:(s + 1) * LANES]
            if rope and cg == 0:
                partner = jnp.where(first_half,
                                    pltpu.roll(slab, LANES - half, axis=1),
                                    pltpu.roll(slab, half, axis=1))
                slab = slab * cos_ref[...] + partner * sin_ref[...]
            pt_ref[cg * (COL_GROUP // LANES) + s] = slab.astype(BF16)

    d_grp = COL_GROUP
    part_scale = {0: q_scale, 2: q_scale * LOG2E}
    if rope:
        cos_t = cos_ref[...].T[0:half]
        sin_t = sin_ref[...].T[half:HEAD_DIM]
    n_parts = wf_ref.shape[0] // d_grp
    n_gate = wf_ref.shape[0] - n_parts * d_grp
    for part in range(n_parts):
        last = part == n_parts - 1
        acc = _dot_nt(wf_ref[part * d_grp:(part + 1) * d_grp + (n_gate if last else 0), :], hb)
        if last:
            _in_proj_gates(acc[d_grp:] + bg_ref[:, 0:1], gc_ref, gr_ref, nh)
        for hh in range(d_grp // HEAD_DIM):
            r0 = hh * HEAD_DIM
            if part == 0 and rope:
                x1 = acc[r0:r0 + half]
                x2 = acc[r0 + half:r0 + HEAD_DIM]
                blk = jnp.concatenate([x1 * cos_t - x2 * sin_t, x1 * sin_t + x2 * cos_t], axis=0)
            else:
                blk = acc[r0:r0 + HEAD_DIM]
            if part in part_scale:
                blk = blk * part_scale[part]
            blk = blk.astype(BF16)
            for c in range(n_chunks):
                pf_ref[c, part * d_grp + r0:part * d_grp + r0 + HEAD_DIM, :] = blk[:, c * L:(c + 1) * L]


def _in_proj(x, mod4, mod_row, g, w_tok, w_feat, b_gate, cos_t, sin_t, *, rope, nh):
    b, t, d = x.shape
    nt = w_tok.shape[1]
    nfw = w_feat.shape[0]
    nf = nfw - 4 * nh
    tm = min(TOKEN_TILE, t)
    nc = tm // MLSTM_L
    kern = functools.partial(_in_proj_kernel, d_model=d, rope=rope, nh=nh)
    const = lambda i, bi: (0, 0)
    return pl.pallas_call(
        kern,
        grid=(t // tm, b),
        in_specs=[
            pl.BlockSpec((None, tm, d), lambda i, bi: (bi, i, 0)),
            pl.BlockSpec((None, 1, N_MOD * d), lambda i, bi: (mod_row(bi), 0, 0)),
            pl.BlockSpec((1, d), const),
            pl.BlockSpec((d, nt), const, pipeline_mode=pl.Buffered(1)),
            pl.BlockSpec((nfw, d), const, pipeline_mode=pl.Buffered(1)),
            pl.BlockSpec((4 * nh, LANES), const),
            pl.BlockSpec((tm, LANES), lambda i, bi: (i, 0)),
            pl.BlockSpec((tm, LANES), lambda i, bi: (i, 0)),
        ],
        out_specs=[
            pl.BlockSpec((None, nt // LANES, tm, LANES), lambda i, bi: (bi, 0, i, 0)),
            pl.BlockSpec((None, nc, nf, MLSTM_L), lambda i, bi: (bi, i, 0, 0)),
            pl.BlockSpec((None, tm, LANES), lambda i, bi: (bi, i, 0)),
            pl.BlockSpec((None, nc, 8 * nh, MLSTM_L), lambda i, bi: (bi, i, 0, 0)),
        ],
        out_shape=[
            jax.ShapeDtypeStruct((b, nt // LANES, t, LANES), BF16),
            jax.ShapeDtypeStruct((b, t // MLSTM_L, nf, MLSTM_L), BF16),
            jax.ShapeDtypeStruct((b, t, LANES), F32),
            jax.ShapeDtypeStruct((b, t // MLSTM_L, 8 * nh, MLSTM_L), F32),
        ],
        compiler_params=pltpu.CompilerParams(
            dimension_semantics=("arbitrary", "arbitrary"), vmem_limit_bytes=VMEM_LIMIT),
        name="in_proj_rope" if rope else "in_proj",
    )(x, mod4, g, w_tok, w_feat, b_gate, cos_t, sin_t)


def _chunk_rows(j):
    if isinstance(j, int):
        return pl.ds(j * MLSTM_L, MLSTM_L)
    return pl.ds(pl.multiple_of(j * MLSTM_L, MLSTM_L), MLSTM_L)


N_REP = 16
AUG = HEAD_DIM + N_REP


def _mlstm_intra(qt_ref, vt_ref, k_ref, gc_ref, gr_ref, j, tau, hp, m_state, p_ref, rv_ref, u_ref,
                 *, rev, nh):
    L = MLSTM_L
    d = HEAD_DIM
    rows = _chunk_rows(j)
    gcb = gc_ref[rows, :]
    qt = qt_ref[j]
    k2 = k_ref[rows, :]
    lane = lax.broadcasted_iota(jnp.int32, (L, LANES), 1)
    sub = lax.broadcasted_iota(jnp.int32, (L, LANES), 0)
    tri = (sub >= lane) if rev else (sub <= lane)
    zero = jnp.zeros_like(qt)
    qbd = jnp.concatenate([jnp.where(sub < d, qt, zero), jnp.where(sub >= d, qt, zero)], axis=1)
    st = _dot(k2, qbd)
    base = nh if rev else 0
    m_new, ws = [], []
    for hd in range(2):
        r = base + 2 * hp + hd
        a_col = jnp.sum(jnp.where(lane == r, gcb, 0.0), axis=1, keepdims=True)
        a_row = gr_ref[j, pl.ds(r, 1), :]
        b_row = gr_ref[j, pl.ds(2 * nh + r, 1), :]
        tot = gr_ref[j, pl.ds(4 * nh + r, 1), :]
        amax = gr_ref[j, pl.ds(6 * nh + r, 1), :]
        m_h = m_state[hd]
        at = jnp.where(tri, a_col, NEG)
        mrow = jnp.maximum(jnp.max(at, axis=0, keepdims=True), m_h)
        p_ref[int(rev), tau, :, hd * L:(hd + 1) * L] = (
            st[:, hd * L:(hd + 1) * L] * jnp.exp2(at - mrow)).astype(BF16)
        m_end = jnp.maximum(m_h, amax)
        ws.append(jnp.exp2(a_row - m_end))
        rv_ref[int(rev), tau, 4 * hd:4 * hd + 4, :] = jnp.concatenate([
            jnp.exp2(m_h - mrow),
            jnp.exp2(-(b_row + mrow)),
            ws[hd],
            jnp.exp2(m_h - m_end),
        ], axis=0)
        m_new.append(tot + m_end)
    vf = vt_ref[j].astype(F32)
    vsc = jnp.concatenate(
        [(vf[hd * d:(hd + 1) * d] * ws[hd]).astype(BF16) for hd in range(2)]
        + [jnp.broadcast_to(ws[hd], (N_REP, L)).astype(BF16) for hd in range(2)], axis=0)
    u_ref[int(rev), tau] = jnp.where(_own_features(), _dot(vsc, k2), 0.0)
    return m_new


def _head_a_rows():
    row = lax.broadcasted_iota(jnp.int32, (2 * AUG, LANES), 0)
    return (row < HEAD_DIM) | ((row >= 2 * HEAD_DIM) & (row < 2 * HEAD_DIM + N_REP))


def _own_features():
    lane = lax.broadcasted_iota(jnp.int32, (2 * AUG, LANES), 1)
    return _head_a_rows() == (lane < HEAD_DIM)


def _mlstm_chunk(qt_ref, vt_ref, j, tau, ct_ref, p_ref, rv_ref, u_ref, hacc_ref, *, rev):
    L = MLSTM_L
    d = HEAD_DIM
    qt = qt_ref[j]
    vt = vt_ref[j]
    rv = rv_ref[int(rev), tau]
    wprev, eneg, _, wc = ([rv[4 * hd + i:4 * hd + i + 1] for hd in range(2)] for i in range(4))
    qf = qt.astype(F32)
    qw = jnp.concatenate([(qf * wprev[0]).astype(BF16), (qf * wprev[1]).astype(BF16)], axis=1)
    rhs = jnp.concatenate([p_ref[int(rev), tau], qw], axis=0)
    ones = jnp.ones((2 * N_REP, L), BF16)
    lhs = jnp.concatenate([jnp.concatenate([vt, ones], axis=0), ct_ref[...].astype(BF16)], axis=1)
    res = _dot(lhs, rhs)
    hs = []
    for hd in range(2):
        num = res[hd * d:(hd + 1) * d, hd * L:(hd + 1) * L]
        den = res[2 * d + hd * N_REP:2 * d + hd * N_REP + 1, hd * L:(hd + 1) * L]
        hs.append(num * (1.0 / jnp.maximum(jnp.abs(den), eneg[hd])))
    hacc_ref[int(rev), j] = jnp.concatenate(hs, axis=0)
    ct_ref[...] = jnp.where(_head_a_rows(), wc[0], wc[1]) * ct_ref[...] + u_ref[int(rev), tau]


def _mlstm_finish(hacc_ref, o_ref, g_ref, y_ref, n_chunks):
    sub = lax.broadcasted_iota(jnp.int32, (LANES, MLSTM_L), 0)
    row_a = sub < HEAD_DIM

    def head_mean(v):
        ma = jnp.mean(v[:HEAD_DIM], axis=0, keepdims=True)
        mb = jnp.mean(v[HEAD_DIM:], axis=0, keepdims=True)
        return jnp.where(row_a, ma, mb)

    def body(j, carry):
        rows = _chunk_rows(j)
        hs = hacc_ref[0, j] + hacc_ref[1, j]
        dv = hs - head_mean(hs)
        yn = (dv * lax.rsqrt(head_mean(dv * dv) + EPS)).T
        gate = _sigmoid(o_ref[rows, :].astype(F32))
        y_ref[rows, :] = (yn * g_ref[...] * gate).astype(y_ref.dtype)
        return carry

    lax.fori_loop(0, n_chunks, body, 0, unroll=min(4, n_chunks))


def _mlstm_kernel(qtx_ref, vtx_ref, kx_ref, ox_ref, gcx_ref, grx_ref,
                  qtc_ref, vtc_ref, kc_ref, oc_ref, gcc_ref, grc_ref, g_ref, *rest, need_ctx, nh):
    if need_ctx:
        y_ref, yc_ref, cf_ref, cb_ref, p_ref, rv_ref, u_ref, hx_ref, hc_ref = rest
    else:
        y_ref, cf_ref, cb_ref, p_ref, rv_ref, u_ref, hx_ref, hc_ref = rest
        yc_ref = None
    hp = pl.program_id(1)
    ncx = qtx_ref.shape[0]
    ncc = qtc_ref.shape[0]
    intra = functools.partial(_mlstm_intra, hp=hp, p_ref=p_ref, rv_ref=rv_ref, u_ref=u_ref, nh=nh)
    zero = jnp.zeros((1, MLSTM_L), F32)
    mf = [zero, zero]
    mb = [zero, zero]
    for j in range(ncc):
        mf = intra(qtc_ref, vtc_ref, kc_ref, gcc_ref, grc_ref, j, j, m_state=mf, rev=False)
        mb = intra(qtc_ref, vtc_ref, kc_ref, gcc_ref, grc_ref, ncc - 1 - j, j, m_state=mb, rev=True)

    def intra_body(j, carry):
        mf_ = intra(qtx_ref, vtx_ref, kx_ref, gcx_ref, grx_ref, j, ncc + j,
                    m_state=list(carry[0:2]), rev=False)
        mb_ = intra(qtx_ref, vtx_ref, kx_ref, gcx_ref, grx_ref, ncx - 1 - j, ncc + j,
                    m_state=list(carry[2:4]), rev=True)
        return (mf_[0], mf_[1], mb_[0], mb_[1])

    lax.fori_loop(0, ncx, intra_body, (mf[0], mf[1], mb[0], mb[1]), unroll=2)

    cf_ref[...] = jnp.zeros_like(cf_ref)
    cb_ref[...] = jnp.zeros_like(cb_ref)
    step = functools.partial(_mlstm_chunk, p_ref=p_ref, rv_ref=rv_ref, u_ref=u_ref)
    for j in range(ncc):
        step(qtc_ref, vtc_ref, j, j, cf_ref, hacc_ref=hc_ref, rev=False)
        step(qtc_ref, vtc_ref, ncc - 1 - j, j, cb_ref, hacc_ref=hc_ref, rev=True)

    def body(j, carry):
        step(qtx_ref, vtx_ref, j, ncc + j, cf_ref, hacc_ref=hx_ref, rev=False)
        step(qtx_ref, vtx_ref, ncx - 1 - j, ncc + j, cb_ref, hacc_ref=hx_ref, rev=True)
        return carry

    lax.fori_loop(0, ncx, body, 0, unroll=4)
    _mlstm_finish(hx_ref, ox_ref, g_ref, y_ref, ncx)
    if need_ctx:
        _mlstm_finish(hc_ref, oc_ref, g_ref, yc_ref, ncc)


def _mlstm(ptx, pfx, gcx, grx, ptc, pfc, gcc, grc, norm_g, *, need_ctx, d_grp):
    b, _, t, _ = ptx.shape
    ct = ptc.shape[2]
    nh = d_grp // HEAD_DIM
    n_pairs = d_grp // LANES
    n_steps = (t + ct) // MLSTM_L

    def seq_specs(tlen, pf, gr):
        nc = tlen // MLSTM_L
        return [
            pl.BlockSpec((None, nc, LANES, MLSTM_L), lambda bi, hp: (bi, 0, hp, 0)),
            pl.BlockSpec((None, nc, LANES, MLSTM_L), lambda bi, hp: (bi, 0, n_pairs + hp, 0)),
            pl.BlockSpec((None, None, tlen, LANES), lambda bi, hp: (bi, hp, 0, 0)),
            pl.BlockSpec((None, None, tlen, LANES), lambda bi, hp: (bi, n_pairs + hp, 0, 0)),
            pl.BlockSpec((None, tlen, LANES), lambda bi, hp: (bi, 0, 0)),
            pl.BlockSpec((None,) + gr.shape[1:], lambda bi, hp: (bi, 0, 0, 0)),
        ]

    out_specs = [pl.BlockSpec((None, None, t, LANES), lambda bi, hp: (bi, hp, 0, 0))]
    out_shape = [jax.ShapeDtypeStruct((b, n_pairs, t, LANES), BF16)]
    if need_ctx:
        out_specs.append(pl.BlockSpec((None, None, ct, LANES), lambda bi, hp: (bi, hp, 0, 0)))
        out_shape.append(jax.ShapeDtypeStruct((b, n_pairs, ct, LANES), BF16))
    kern = functools.partial(_mlstm_kernel, need_ctx=need_ctx, nh=nh)
    res = pl.pallas_call(
        kern,
        grid=(b, n_pairs),
        in_specs=seq_specs(t, pfx, grx) + seq_specs(ct, pfc, grc) + [
            pl.BlockSpec((1, LANES), lambda bi, hp: (0, hp))],
        out_specs=out_specs,
        out_shape=out_shape,
        scratch_shapes=[
            pltpu.VMEM((2 * AUG, LANES), F32),
            pltpu.VMEM((2 * AUG, LANES), F32),
            pltpu.VMEM((2, n_steps, MLSTM_L, 2 * MLSTM_L), BF16),
            pltpu.VMEM((2, n_steps, 8, MLSTM_L), F32),
            pltpu.VMEM((2, n_steps, 2 * AUG, LANES), F32),
            pltpu.VMEM((2, t // MLSTM_L, LANES, MLSTM_L), F32),
            pltpu.VMEM((2, ct // MLSTM_L, LANES, MLSTM_L), F32),
        ],
        compiler_params=pltpu.CompilerParams(
            dimension_semantics=("arbitrary", "arbitrary"), vmem_limit_bytes=VMEM_LIMIT),
        name="mlstm_ctx" if need_ctx else "mlstm",
    )(pfx, pfx, ptx, ptx, gcx, grx, pfc, pfc, ptc, ptc, gcc, grc, norm_g)
    return (res[0], res[1]) if need_ctx else (res[0], None)


NA_KEY_GROUP = 256


def _na_softmax(s_ref, p_ref, idx, n_keys, n_q, col_max):
    sums = []
    for lt in range(n_q // LANES):
        cols = slice(lt * LANES, (lt + 1) * LANES)
        m = col_max[:, cols]
        tot = None
        for kg in range(n_keys // NA_KEY_GROUP):
            r = slice(kg * NA_KEY_GROUP, (kg + 1) * NA_KEY_GROUP)
            p = jnp.exp2(s_ref[idx + (r, cols)] - m)
            p_ref[idx + (r, cols)] = p.astype(BF16)
            sk = jnp.sum(p, axis=0, keepdims=True)
            tot = sk if tot is None else tot + sk
        sums.append(tot)
    return jnp.concatenate(sums, axis=1)


def _na_kernel(qt_ref, vt_ref, k_ref, qtc_ref, vtc_ref, kc_ref, bt_ref, *rest, need_ctx, rows):
    if need_ctx:
        y_ref, yc_ref, s_ref, p_ref, m_ref, l_ref = rest
    else:
        y_ref, s_ref, p_ref, m_ref, l_ref = rest
        yc_ref = None
    w = GRID_W
    L = MLSTM_L
    qn = NA_QROWS * w
    kn = NA_WIN * w
    n_ctx = kc_ref.shape[0]
    nblk = rows // NA_QROWS
    kc = kc_ref[...]
    vtc = jnp.concatenate([vtc_ref[c] for c in range(n_ctx // L)], axis=1)
    sub = lax.broadcasted_iota(jnp.int32, (LANES, qn), 0)
    row_a = sub < HEAD_DIM

    def first_key_row(ib):
        if isinstance(ib, int):
            return min(max(ib * NA_QROWS - NA_ROWS // 2, 0), rows - NA_WIN)
        return jnp.clip(ib * NA_QROWS - NA_ROWS // 2, 0, rows - NA_WIN)

    def token_rows(start, n):
        if isinstance(start, int):
            return pl.ds(start, n)
        return pl.ds(pl.multiple_of(start, w), n)

    def chunks(ref, tok0, n_tok):
        c0 = tok0 // L
        return jnp.concatenate([ref[c0 + c] for c in range(n_tok // L)], axis=1)

    def head_queries(qt, hd):
        return jnp.where(row_a if hd == 0 else jnp.logical_not(row_a), qt, jnp.zeros_like(qt))

    def scores(ib, buf):
        if isinstance(ib, int):
            var = 0 if ib == 0 else (2 if ib == nblk - 1 else 1)
        else:
            var = jnp.where(ib == nblk - 1, 2, 1)
        qt = chunks(qt_ref, ib * qn, qn)
        kw = k_ref[token_rows(first_key_row(ib) * w, kn), :]
        for hd in range(2):
            qh = head_queries(qt, hd)
            s_loc = _dot(kw, qh) + bt_ref[var, hd]
            s_ctx = _dot(kc, qh)
            s_ref[buf, hd, 0:kn, :] = s_loc
            s_ref[buf, hd, kn:, :] = s_ctx
            m_ref[buf, hd] = jnp.broadcast_to(
                jnp.maximum(jnp.max(s_loc, axis=0, keepdims=True), jnp.max(s_ctx, axis=0, keepdims=True)),
                (8, qn))

    def softmax(buf):
        for hd in range(2):
            l_ref[buf, hd] = jnp.broadcast_to(
                _na_softmax(s_ref, p_ref, (buf, hd), kn + n_ctx, qn, m_ref[buf, hd, 0:1, :]), (8, qn))

    def values(ib, buf):
        vt = jnp.concatenate([chunks(vt_ref, first_key_row(ib) * w, kn), vtc], axis=1)
        outs = [_dot(vt, p_ref[buf, hd]) * (1.0 / l_ref[buf, hd, 0:1, :]) for hd in range(2)]
        y_ref[token_rows(ib * qn, qn), :] = jnp.where(row_a, outs[0], outs[1]).T.astype(y_ref.dtype)

    scores(0, 0)
    scores(1, 1)
    softmax(0)

    def body(tt, carry):
        t = 2 * tt + 1
        scores(t + 1, 0)
        softmax(1)
        values(t - 1, 0)
        scores(t + 2, 1)
        softmax(0)
        values(t, 1)
        return carry

    lax.fori_loop(0, (nblk - 2) // 2, body, 0)
    softmax(1)
    values(nblk - 2, 0)
    values(nblk - 1, 1)

    if need_ctx:
        qtc = jnp.concatenate([qtc_ref[c] for c in range(n_ctx // L)], axis=1)
        sub_c = lax.broadcasted_iota(jnp.int32, qtc.shape, 0)
        row_ac = sub_c < HEAD_DIM
        for c0 in range(0, n_ctx, qn):
            outs = []
            for hd in range(2):
                qh = jnp.where(row_ac if hd == 0 else jnp.logical_not(row_ac), qtc, jnp.zeros_like(qtc))
                s_cc = _dot(kc, qh[:, c0:c0 + qn])
                s_ref[0, hd, 0:n_ctx, :] = s_cc
                lsum = _na_softmax(s_ref, p_ref, (0, hd), n_ctx, qn, jnp.max(s_cc, axis=0, keepdims=True))
                outs.append(_dot(vtc, p_ref[0, hd, 0:n_ctx, :]) * (1.0 / lsum))
            yc_ref[c0:c0 + qn, :] = jnp.where(row_a, outs[0], outs[1]).T.astype(yc_ref.dtype)


def _na_bias_kernel(r_ref, o_ref):
    w = GRID_W
    lane = lax.broadcasted_iota(jnp.int32, (w, LANES), 1)
    k = lax.broadcasted_iota(jnp.int32, (w, LANES), 0)
    col_start = jnp.clip(lane % w - NA_COLS // 2, 0, w - NA_COLS)
    col_ok = (k >= col_start) & (k < col_start + NA_COLS)
    left = lane < w
    lane_row = lax.broadcasted_iota(jnp.int32, (1, LANES), 1)
    neg = jnp.full((w, LANES), NEG, F32)
    n_dr = 2 * NA_ROWS - 1
    cache = {}

    def pair_tile(dr_a, ok_a, ok_b):
        key = (dr_a, ok_a, ok_b)
        if key not in cache:
            if not (ok_a or ok_b):
                cache[key] = neg
            else:
                ia = min(max(dr_a, 0), n_dr - 1)
                ib = min(max(dr_a - 1, 0), n_dr - 1)
                src = jnp.where(lane_row < w, r_ref[ia:ia + 1, :], pltpu.roll(r_ref[ib:ib + 1, :], w, axis=1))
                toep = pltpu.roll(jnp.broadcast_to(src, (w, LANES)), LANES - (NA_COLS - 1), axis=1,
                                  stride=1, stride_axis=0)
                ok = col_ok
                if not ok_a:
                    ok = ok & jnp.logical_not(left)
                if not ok_b:
                    ok = ok & left
                cache[key] = jnp.where(ok, toep * LOG2E, neg)
        return cache[key]

    offs = (0, NA_ROWS // 2, NA_ROWS)
    for v in range(3):
        for j in range(NA_WIN):
            for ii in range(NA_QROWS // 2):
                i = 2 * ii
                lo_a, lo_b = ((0, 0), (i, i + 1), (NA_QROWS, NA_QROWS))[v]
                dr_a = j - i + (NA_ROWS - 1) - offs[v]
                ok_a = lo_a <= j < lo_a + NA_ROWS
                ok_b = lo_b <= j < lo_b + NA_ROWS
                o_ref[v, j * w:(j + 1) * w, ii * LANES:(ii + 1) * LANES] = pair_tile(dr_a, ok_a, ok_b)


def _na_bias_table(rpb):
    depth, h, nr, nc = rpb.shape
    assert nr == 2 * NA_ROWS - 1 and nc == 2 * NA_COLS - 1 and 2 * GRID_W == LANES
    r_pad = jnp.pad(rpb.astype(F32)[..., ::-1], ((0, 0), (0, 0), (0, 16 - nr), (0, LANES - nc)))
    qn, kn = NA_QROWS * GRID_W, NA_WIN * GRID_W
    return pl.pallas_call(
        _na_bias_kernel,
        grid=(depth, h),
        in_specs=[pl.BlockSpec((None, None, 16, LANES), lambda l, hh: (l, hh, 0, 0))],
        out_specs=pl.BlockSpec((None, 3, None, kn, qn), lambda l, hh: (l, 0, hh, 0, 0)),
        out_shape=jax.ShapeDtypeStruct((depth, 3, h, kn, qn), F32),
        compiler_params=pltpu.CompilerParams(
            dimension_semantics=("arbitrary", "arbitrary"), vmem_limit_bytes=VMEM_LIMIT),
        name="natten_bias",
    )(r_pad)


def _na(ptx, pfx, ptc, pfc, btab, *, need_ctx, d_grp, k_group, q_group, v_group):
    b, _, t, _ = ptx.shape
    ct = ptc.shape[2]
    n_pairs = d_grp // LANES
    rows = t // GRID_W
    qn, kn = NA_QROWS * GRID_W, NA_WIN * GRID_W + ct
    assert (rows // NA_QROWS) % 2 == 0 and rows // NA_QROWS >= 4
    assert ct % qn == 0 and ct % NA_KEY_GROUP == 0 and kn % NA_KEY_GROUP == 0 and qn % MLSTM_L == 0

    def seq_specs(tlen):
        nc = tlen // MLSTM_L
        return [
            pl.BlockSpec((None, nc, LANES, MLSTM_L), lambda hp, bi: (bi, 0, q_group + hp, 0)),
            pl.BlockSpec((None, nc, LANES, MLSTM_L), lambda hp, bi: (bi, 0, v_group + hp, 0)),
            pl.BlockSpec((None, None, tlen, LANES), lambda hp, bi: (bi, k_group + hp, 0, 0)),
        ]

    in_specs = seq_specs(t) + seq_specs(ct) + [
        pl.BlockSpec((3, 2) + btab.shape[2:], lambda hp, bi: (0, hp, 0, 0))]
    out_specs = [pl.BlockSpec((None, None, t, LANES), lambda hp, bi: (bi, hp, 0, 0))]
    out_shape = [jax.ShapeDtypeStruct((b, n_pairs, t, LANES), BF16)]
    if need_ctx:
        out_specs.append(pl.BlockSpec((None, None, ct, LANES), lambda hp, bi: (bi, hp, 0, 0)))
        out_shape.append(jax.ShapeDtypeStruct((b, n_pairs, ct, LANES), BF16))
    kern = functools.partial(_na_kernel, need_ctx=need_ctx, rows=rows)
    res = pl.pallas_call(
        kern,
        grid=(n_pairs, b),
        in_specs=in_specs,
        out_specs=out_specs,
        out_shape=out_shape,
        scratch_shapes=[
            pltpu.VMEM((2, 2, kn, qn), F32),
            pltpu.VMEM((2, 2, kn, qn), BF16),
            pltpu.VMEM((2, 2, 8, qn), F32),
            pltpu.VMEM((2, 2, 8, qn), F32),
        ],
        compiler_params=pltpu.CompilerParams(
            dimension_semantics=("arbitrary", "arbitrary"), vmem_limit_bytes=VMEM_LIMIT),
        name="natten_ctx" if need_ctx else "natten",
    )(pfx, pfx, ptx, pfc, pfc, ptc, btab)
    return (res[0], res[1]) if need_ctx else (res[0], None)


def _out_mlp_kernel(x_ref, ym_ref, yn_ref, mod_ref, g_ref, wo_ref, w1_ref, w2_ref, fg_ref, o_ref,
                    *, d_model, final_norm):
    d = d_model
    y = jnp.concatenate([r[g] for r in (ym_ref, yn_ref) for g in range(r.shape[0])], axis=1)
    att = _dot(y, wo_ref[...])
    x1 = x_ref[...] + mod_ref[:, 2 * d:3 * d] * att
    h = _modulated_norm(x1, g_ref[...], mod_ref[:, 4 * d:5 * d], mod_ref[:, 3 * d:4 * d]).astype(BF16)
    d_ff = w1_ref.shape[1]
    acc = None
    for c in range(d_ff // COL_GROUP):
        hid = _dot(h, w1_ref[:, c * COL_GROUP:(c + 1) * COL_GROUP])
        hid = jnp.square(jnp.maximum(hid, 0.0)).astype(BF16)
        part = _dot(hid, w2_ref[c * COL_GROUP:(c + 1) * COL_GROUP, :])
        acc = part if acc is None else acc + part
    x2 = x1 + mod_ref[:, 5 * d:6 * d] * acc
    if final_norm:
        x2 = (x2 * lax.rsqrt(jnp.mean(x2 * x2, axis=-1, keepdims=True) + EPS)) * fg_ref[...]
    o_ref[...] = x2


def _out_mlp(x, ym, yn, mod4, mod_row, g, w_out, w1, w2, final_g, *, final_norm):
    b, t, d = x.shape
    n_pairs = ym.shape[1]
    d_ff = w1.shape[1]
    tm = min(TOKEN_TILE, t)
    kern = functools.partial(_out_mlp_kernel, d_model=d, final_norm=final_norm)
    const = lambda i, bi: (0, 0)
    return pl.pallas_call(
        kern,
        grid=(t // tm, b),
        in_specs=[
            pl.BlockSpec((None, tm, d), lambda i, bi: (bi, i, 0)),
            pl.BlockSpec((None, n_pairs, tm, LANES), lambda i, bi: (bi, 0, i, 0)),
            pl.BlockSpec((None, n_pairs, tm, LANES), lambda i, bi: (bi, 0, i, 0)),
            pl.BlockSpec((None, 1, N_MOD * d), lambda i, bi: (mod_row(bi), 0, 0)),
            pl.BlockSpec((1, d), const),
            pl.BlockSpec((d, d), const, pipeline_mode=pl.Buffered(1)),
            pl.BlockSpec((d, d_ff), const, pipeline_mode=pl.Buffered(1)),
            pl.BlockSpec((d_ff, d), const, pipeline_mode=pl.Buffered(1)),
            pl.BlockSpec((1, d), const),
        ],
        out_specs=pl.BlockSpec((None, tm, d), lambda i, bi: (bi, i, 0)),
        out_shape=jax.ShapeDtypeStruct((b, t, d), F32),
        compiler_params=pltpu.CompilerParams(
            dimension_semantics=("arbitrary", "arbitrary"), vmem_limit_bytes=VMEM_LIMIT),
        name="out_mlp_final" if final_norm else "out_mlp",
    )(x, ym, yn, mod4, g, w_out, w1, w2, final_g)


def _rope_tables(n_tokens):
    t = jnp.arange(n_tokens)
    row = (t // GRID_W).astype(F32)
    col = (t % GRID_W).astype(F32)
    n_freq = HEAD_DIM // 4
    inv_freq = ROPE_BASE ** (-jnp.arange(n_freq, dtype=F32) / n_freq)
    ang = jnp.concatenate([row[:, None] * inv_freq, col[:, None] * inv_freq], axis=-1)
    cos, sin = jnp.cos(ang), jnp.sin(ang)
    reps = LANES // HEAD_DIM
    return (jnp.tile(jnp.concatenate([cos, cos], axis=-1), (1, reps)),
            jnp.tile(jnp.concatenate([-sin, sin], axis=-1), (1, reps)))


def kernel(x, c, ctx, c_ctx, w_ada, b_ada, norm1_g, w_in, b_gate, mlstm_norm_g, rpb, w_out, norm2_g,
           w_mlp1, w_mlp2, final_g):
    b, s, d = x.shape
    ct = ctx.shape[1]
    depth = w_ada.shape[0]
    d_grp = d // 2
    nh = d_grp // HEAD_DIM
    n_gate = b_gate.shape[1]
    n_main = w_in.shape[2] - n_gate
    assert n_main == 7 * d_grp and d_grp % LANES == 0 and n_gate == 4 * nh and n_gate <= LANES
    assert s % TOKEN_TILE == 0 and ct % MLSTM_L == 0 and (s // GRID_W) % NA_QROWS == 0

    n_rows = -(-(b + 1) // 8) * 8
    cc = jnp.zeros((n_rows, d), F32).at[:b].set(c).at[b].set(c_ctx)
    mod = _adaln(cc, w_ada, b_ada)
    mod = mod.reshape(depth, n_rows, 1, N_MOD * d)

    cos_t, sin_t = _rope_tables(s)
    grp = lambda k: w_in[:, :, k * d_grp:(k + 1) * d_grp]
    w_tok = jnp.concatenate([grp(1), grp(3), grp(5)], axis=2).astype(BF16)
    w_feat = jnp.swapaxes(
        jnp.concatenate([grp(0), grp(2), grp(4), grp(6), w_in[:, :, n_main:]], axis=2), 1, 2).astype(BF16)
    b_gate_p = jnp.broadcast_to(b_gate.astype(F32)[:, :, None], (depth, n_gate, LANES))
    w_out_b = w_out.astype(BF16)
    w1_b = w_mlp1.astype(BF16)
    w2_b = w_mlp2.astype(BF16)
    n_pairs = d_grp // LANES
    btab = _na_bias_table(rpb)

    x_row = lambda bi: bi
    c_row = lambda bi: b
    xc = ctx
    for l in range(depth):
        need_ctx = l < depth - 1
        g1 = norm1_g[l].reshape(1, d)
        g2 = norm2_g[l].reshape(1, d)
        ptx, pfx, gcx, grx = _in_proj(x, mod[l], x_row, g1, w_tok[l], w_feat[l], b_gate_p[l],
                                      cos_t, sin_t, rope=True, nh=nh)
        ptc, pfc, gcc, grc = _in_proj(xc, mod[l], c_row, g1, w_tok[l], w_feat[l], b_gate_p[l],
                                      cos_t[:ct], sin_t[:ct], rope=False, nh=nh)
        ym, ycm = _mlstm(ptx, pfx, gcx, grx, ptc, pfc, gcc, grc, mlstm_norm_g[l].reshape(1, d_grp),
                         need_ctx=need_ctx, d_grp=d_grp)
        yn, ycn = _na(ptx, pfx, ptc, pfc, btab[l], need_ctx=need_ctx, d_grp=d_grp,
                      k_group=2 * n_pairs, q_group=2 * n_pairs, v_group=3 * n_pairs)
        fg = final_g.reshape(1, d)
        x = _out_mlp(x, ym, yn, mod[l], x_row, g2, w_out_b[l], w1_b[l], w2_b[l], fg,
                     final_norm=(l == depth - 1))
        if need_ctx:
            xc = _out_mlp(xc, ycm, ycn, mod[l], c_row, g2, w_out_b[l], w1_b[l], w2_b[l], fg,
                          final_norm=False)
    return x
```
